```python
import jax, jax.numpy as jnp
from jax import lax
import numpy as np

D_MODEL = 2048
BATCH = 4
SEQ = 4096
DEPTH = 4

CHUNK = 64
Q_BLOCK = 128
HEAD_DIM = 128
N_HEADS_FOX = 8
N_HEADS_SB = 8
WIDTH_FOX = N_HEADS_FOX * HEAD_DIM
WIDTH_SB = N_HEADS_SB * HEAD_DIM
D_IN = 3 * WIDTH_FOX + N_HEADS_FOX + 3 * WIDTH_SB
D_FF = 5632
N_EXPERTS = 8
TOP_K = 2
D_FF_EXPERT = 5632
MOE_BLOCK = 256
N_DENSE = (DEPTH + 1) // 2
N_MOE = DEPTH // 2
RMS_EPS = 1e-6

kernel_name = "fox_stickbreak_gated_hybrid_moe"


def rms_norm(x, g):
    xf = x.astype(jnp.float32)
    y = xf * lax.rsqrt(jnp.mean(xf * xf, axis=-1, keepdims=True) + RMS_EPS)
    return (y * g.astype(jnp.float32)).astype(x.dtype)


def to_heads(t, n_heads):
    b, s, _ = t.shape
    return t.reshape(b, s, n_heads, HEAD_DIM).transpose(0, 2, 1, 3)


def from_heads(t):
    b, h, s, d = t.shape
    return t.transpose(0, 2, 1, 3).reshape(b, s, h * d)


def query_blocks(t):
    b, h, s = t.shape[:3]
    nb = s // Q_BLOCK
    t = t.reshape((b, h, nb, Q_BLOCK) + t.shape[3:])
    return jnp.moveaxis(t, 2, 0)


def unblock(o):
    nb, b, h, qb, d = o.shape
    return jnp.moveaxis(o, 0, 2).reshape(b, h, nb * qb, d)


def forgetting_attention(q, k, v, log_f):
    s_len = q.shape[2]
    nb = s_len // Q_BLOCK
    dcum = jnp.cumsum(log_f, axis=-1)
    kpos = jnp.arange(s_len)
    scale = HEAD_DIM ** -0.5

    def one_block(args):
        qi, di, i = args
        qpos = i * Q_BLOCK + jnp.arange(Q_BLOCK)
        logits = jnp.einsum('bhqd,bhkd->bhqk', qi, k,
                            preferred_element_type=jnp.float32) * scale
        logits = logits + di[..., :, None] - dcum[..., None, :]
        mask = kpos[None, :] <= qpos[:, None]
        p = jax.nn.softmax(jnp.where(mask, logits, -jnp.inf), axis=-1)
        return jnp.einsum('bhqk,bhkd->bhqd', p.astype(v.dtype), v)

    o = lax.map(one_block, (query_blocks(q), query_blocks(dcum), jnp.arange(nb)))
    return unblock(o)


def stick_breaking_attention(q, k, v):
    s_len = q.shape[2]
    nb = s_len // Q_BLOCK
    kpos = jnp.arange(s_len)
    scale = HEAD_DIM ** -0.5

    def one_block(args):
        qi, i = args
        qpos = i * Q_BLOCK + jnp.arange(Q_BLOCK)
        z = jnp.einsum('bhqd,bhkd->bhqk', qi, k,
                       preferred_element_type=jnp.float32) * scale
        mask = kpos[None, :] < qpos[:, None]
        log_beta = jax.nn.log_sigmoid(z)
        log_1m_beta = jnp.where(mask, jax.nn.log_sigmoid(-z), 0.0)
        tail = lax.cumsum(log_1m_beta, axis=log_1m_beta.ndim - 1, reverse=True) - log_1m_beta
        a = jnp.where(mask, jnp.exp(log_beta + tail), 0.0)
        return jnp.einsum('bhqk,bhkd->bhqd', a.astype(v.dtype), v)

    o = lax.map(one_block, (query_blocks(q), jnp.arange(nb)))
    return unblock(o)


def head_rms_norm(t, g):
    tf = t.astype(jnp.float32)
    y = tf * lax.rsqrt(jnp.mean(tf * tf, axis=-1, keepdims=True) + RMS_EPS)
    return (y * g.astype(jnp.float32)[None, :, None, :]).astype(t.dtype)


def hybrid_mixer(h, w_in, qn_g, kn_g, b_f, w_o_fox, w_o_sb, w_gate, b_gate, w_out):
    proj = h @ w_in
    offs = np.cumsum([WIDTH_FOX, WIDTH_FOX, WIDTH_FOX, N_HEADS_FOX, WIDTH_SB, WIDTH_SB]).tolist()
    q_f, k_f, v_f, f_logit, q_s, k_s, v_s = jnp.split(proj, offs, axis=-1)

    qf = head_rms_norm(to_heads(q_f, N_HEADS_FOX), qn_g)
    kf = head_rms_norm(to_heads(k_f, N_HEADS_FOX), kn_g)
    vf = to_heads(v_f, N_HEADS_FOX)
    log_f = jax.nn.log_sigmoid(f_logit.astype(jnp.float32) + b_f.astype(jnp.float32))
    o_fox = from_heads(forgetting_attention(qf, kf, vf, log_f.transpose(0, 2, 1)))

    o_sb = from_heads(stick_breaking_attention(
        to_heads(q_s, N_HEADS_SB), to_heads(k_s, N_HEADS_SB), to_heads(v_s, N_HEADS_SB)))

    g_fox, g_sb = jnp.split(jax.nn.sigmoid(h @ w_gate + b_gate), 2, axis=-1)
    merged = g_fox * (o_fox @ w_o_fox) + g_sb * (o_sb @ w_o_sb)
    return merged @ w_out


def swiglu(h, w1, w3, w2):
    return (jax.nn.silu(h @ w1) * (h @ w3)) @ w2


def moe_swiglu(h, w_router, w1, w3, w2):
    b, s, d = h.shape
    n_tok = b * s
    n_assign = n_tok * TOP_K
    xf = h.reshape(n_tok, d)
    logits = (xf @ w_router).astype(jnp.float32)
    top_logit, top_idx = lax.top_k(logits, TOP_K)
    top_w = jax.nn.softmax(top_logit, axis=-1)

    flat_e = top_idx.reshape(-1)
    flat_tok = jnp.repeat(jnp.arange(n_tok, dtype=jnp.int32), TOP_K)
    flat_w = top_w.reshape(-1)
    order = jnp.argsort(flat_e)
    sorted_e = flat_e[order]
    sorted_tok = flat_tok[order]
    sorted_w = flat_w[order]

    counts = jnp.bincount(flat_e, length=N_EXPERTS)
    starts = jnp.cumsum(counts) - counts
    padded = (counts + MOE_BLOCK - 1) // MOE_BLOCK * MOE_BLOCK
    pad_ends = jnp.cumsum(padded)
    pad_starts = pad_ends - padded
    rank = jnp.arange(n_assign) - starts[sorted_e]
    dest = pad_starts[sorted_e] + rank

    n_blocks = -(-n_assign // MOE_BLOCK) + N_EXPERTS
    n_slots = n_blocks * MOE_BLOCK
    slot_tok = jnp.zeros((n_slots,), jnp.int32).at[dest].set(sorted_tok)
    slot_w = jnp.zeros((n_slots,), jnp.float32).at[dest].set(sorted_w)
    block_e = jnp.minimum(
        jnp.searchsorted(pad_ends, jnp.arange(n_blocks) * MOE_BLOCK, side='right'),
        N_EXPERTS - 1)

    xs = xf[slot_tok].reshape(n_blocks, MOE_BLOCK, d)

    def expert_block(args):
        xb, e = args
        return (jax.nn.silu(xb @ w1[e]) * (xb @ w3[e])) @ w2[e]

    yb = lax.map(expert_block, (xs, block_e)).reshape(n_slots, d)
    y = yb * slot_w[:, None].astype(yb.dtype)
    out = jax.ops.segment_sum(y, slot_tok, num_segments=n_tok)
    return out.reshape(b, s, d)


def setup_inputs(seed: int = 0) -> dict:
    key = jax.random.key(seed)
    ks = jax.random.split(key, 24)
    f32 = jnp.float32

    def nrm(k, shape, fan_in):
        return jax.random.normal(k, shape, f32) * (fan_in ** -0.5)

    def gain(k, shape):
        return 1.0 + 0.01 * jax.random.normal(k, shape, f32)

    fb_base = jnp.linspace(1.0, 4.0, N_HEADS_FOX, dtype=f32)[None, :]
    return {
        "x": jax.random.normal(ks[0], (BATCH, SEQ, D_MODEL), f32),
        "norm1_g": gain(ks[1], (DEPTH, D_MODEL)),
        "w_in": nrm(ks[2], (DEPTH, D_MODEL, D_IN), D_MODEL),
        "qn_g": gain(ks[3], (DEPTH, N_HEADS_FOX, HEAD_DIM)),
        "kn_g": gain(ks[4], (DEPTH, N_HEADS_FOX, HEAD_DIM)),
        "b_f": fb_base + 0.1 * jax.random.normal(ks[5], (DEPTH, N_HEADS_FOX), f32),
        "w_o_fox": nrm(ks[6], (DEPTH, WIDTH_FOX, D_MODEL), WIDTH_FOX),
        "w_o_sb": nrm(ks[7], (DEPTH, WIDTH_SB, D_MODEL), WIDTH_SB),
        "w_gate": nrm(ks[8], (DEPTH, D_MODEL, 2 * D_MODEL), D_MODEL),
        "b_gate": 0.01 * jax.random.normal(ks[9], (DEPTH, 2 * D_MODEL), f32),
        "w_out": nrm(ks[10], (DEPTH, D_MODEL, D_MODEL), D_MODEL),
        "norm2_g": gain(ks[11], (DEPTH, D_MODEL)),
        "ffn_w1": nrm(ks[12], (N_DENSE, D_MODEL, D_FF), D_MODEL),
        "ffn_w3": nrm(ks[13], (N_DENSE, D_MODEL, D_FF), D_MODEL),
        "ffn_w2": nrm(ks[14], (N_DENSE, D_FF, D_MODEL), D_FF),
        "w_router": nrm(ks[15], (N_MOE, D_MODEL, N_EXPERTS), D_MODEL),
        "moe_w1": nrm(ks[16], (N_MOE, N_EXPERTS, D_MODEL, D_FF_EXPERT), D_MODEL),
        "moe_w3": nrm(ks[17], (N_MOE, N_EXPERTS, D_MODEL, D_FF_EXPERT), D_MODEL),
        "moe_w2": nrm(ks[18], (N_MOE, N_EXPERTS, D_FF_EXPERT, D_MODEL), D_FF_EXPERT),
    }


def reference(x, norm1_g, w_in, qn_g, kn_g, b_f, w_o_fox, w_o_sb, w_gate, b_gate,
              w_out, norm2_g, ffn_w1, ffn_w3, ffn_w2, w_router, moe_w1, moe_w3, moe_w2):
    for layer in range(DEPTH):
        h = rms_norm(x, norm1_g[layer])
        x = x + hybrid_mixer(h, w_in[layer], qn_g[layer], kn_g[layer], b_f[layer],
                             w_o_fox[layer], w_o_sb[layer], w_gate[layer], b_gate[layer],
                             w_out[layer])
        h = rms_norm(x, norm2_g[layer])
        i = layer // 2
        if layer % 2 == 0:
            x = x + swiglu(h, ffn_w1[i], ffn_w3[i], ffn_w2[i])
        else:
            x = x + moe_swiglu(h, w_router[i], moe_w1[i], moe_w3[i], moe_w2[i])
    return x
```

```python
import functools

import jax
import jax.numpy as jnp
from jax import lax
from jax.experimental import pallas as pl
from jax.experimental.pallas import tpu as pltpu

HEAD_DIM = 128
RMS_EPS = 1e-6
TOP_K = 2
LANES = 128
NEG_BIG = -1e30
VMEM_LIMIT_CAP = 60000 * 1024

F32 = jnp.float32
BF16 = jnp.bfloat16


def _cparams(semantics, vmem_bytes):
    limit = int(min(VMEM_LIMIT_CAP, max(32 * 1024 * 1024, vmem_bytes)))
    return pltpu.CompilerParams(dimension_semantics=semantics, vmem_limit_bytes=limit)


def _rms_rows(x, g):
    ms = jnp.mean(x * x, axis=-1, keepdims=True)
    return x * lax.rsqrt(ms + RMS_EPS) * g


def _log_sigmoid(z):
    return jnp.minimum(z, 0.0) - jnp.log(1.0 + jnp.exp(-jnp.abs(z)))


def _pick(total, pref):
    t = min(total, pref)
    while total % t:
        t //= 2
    return t


def _proj_kernel(x_ref, g_ref, w_ref, wf_ref, cvec_ref, bvec_ref, bf_ref,
                 y_ref, logf_ref, h_scr, *, n_qk_tiles, n_plain_tiles, tn):
    j = pl.program_id(1)

    @pl.when(j == 0)
    def _():
        hb = _rms_rows(x_ref[...], g_ref[...]).astype(BF16)
        h_scr[...] = hb
        fl = jnp.dot(hb, wf_ref[...], preferred_element_type=F32)
        logf_ref[...] = _log_sigmoid(fl + bf_ref[...])

    acc = jnp.dot(h_scr[...], w_ref[...], preferred_element_type=F32)

    @pl.when(j < n_qk_tiles)
    def _():
        for c in range(tn // HEAD_DIM):
            sl = slice(c * HEAD_DIM, (c + 1) * HEAD_DIM)
            y = acc[:, sl]
            ms = jnp.mean(y * y, axis=-1, keepdims=True)
            y_ref[:, sl] = (y * lax.rsqrt(ms + RMS_EPS) * cvec_ref[:, sl]).astype(BF16)

    @pl.when((j >= n_qk_tiles) & (j < n_qk_tiles + n_plain_tiles))
    def _():
        y_ref[...] = (acc * cvec_ref[...]).astype(BF16)

    @pl.when(j >= n_qk_tiles + n_plain_tiles)
    def _():
        z = acc + bvec_ref[...]
        y_ref[...] = (1.0 / (1.0 + jnp.exp(-z))).astype(BF16)


def _proj(x, g, w, wf, cvec, bvec, bf, *, n_qk_tiles, n_plain_tiles, tm, tn):
    n, d = x.shape
    n_out = w.shape[1]
    kern = functools.partial(_proj_kernel, n_qk_tiles=n_qk_tiles,
                             n_plain_tiles=n_plain_tiles, tn=tn)
    vmem = (2 * tm * d * 4 + tm * d * 2 + 2 * d * tn * 2 + 2 * d * LANES * 2
            + 2 * tm * tn * 2 + 2 * tm * LANES * 4 + 6 * tm * tn * 4 + 3 * tm * d * 4)
    return pl.pallas_call(
        kern,
        grid=(n // tm, n_out // tn),
        in_specs=[
            pl.BlockSpec((tm, d), lambda i, j: (i, 0)),
            pl.BlockSpec((1, d), lambda i, j: (0, 0)),
            pl.BlockSpec((d, tn), lambda i, j: (0, j)),
            pl.BlockSpec((d, LANES), lambda i, j: (0, 0)),
            pl.BlockSpec((1, tn), lambda i, j: (0, j)),
            pl.BlockSpec((1, tn), lambda i, j: (0, j)),
            pl.BlockSpec((1, LANES), lambda i, j: (0, 0)),
        ],
        out_specs=[
            pl.BlockSpec((tm, tn), lambda i, j: (i, j)),
            pl.BlockSpec((tm, LANES), lambda i, j: (i, 0)),
        ],
        out_shape=[
            jax.ShapeDtypeStruct((n, n_out), BF16),
            jax.ShapeDtypeStruct((n, LANES), F32),
        ],
        scratch_shapes=[pltpu.VMEM((tm, d), BF16)],
        compiler_params=_cparams(("parallel", "arbitrary"), vmem),
        name="proj",
    )(x, g, w, wf, cvec, bvec, bf)


def _dcum_kernel(lf_ref, d_ref, dt_ref, *, chunk, n_heads):
    s = lf_ref.shape[0]
    r = lax.broadcasted_iota(jnp.int32, (chunk, chunk), 0)
    c = lax.broadcasted_iota(jnp.int32, (chunk, chunk), 1)
    tri = (c <= r).astype(F32)
    carry = jnp.zeros((1, LANES), F32)
    for i in range(s // chunk):
        sl = slice(i * chunk, (i + 1) * chunk)
        cs = jnp.dot(tri, lf_ref[sl, :], preferred_element_type=F32,
                     precision=lax.Precision.HIGHEST) + carry
        d_ref[sl, :] = cs
        dt_ref[:, sl] = cs.T[:n_heads, :]
        carry = cs[chunk - 1:chunk, :]


def _dcum(logf, *, batch, seq, n_heads):
    chunk = _pick(seq, 256)
    kern = functools.partial(_dcum_kernel, chunk=chunk, n_heads=n_heads)
    return pl.pallas_call(
        kern,
        grid=(batch,),
        in_specs=[pl.BlockSpec((seq, LANES), lambda b: (b, 0))],
        out_specs=[
            pl.BlockSpec((seq, LANES), lambda b: (b, 0)),
            pl.BlockSpec((None, n_heads, seq), lambda b: (b, 0, 0)),
        ],
        out_shape=[
            jax.ShapeDtypeStruct((batch * seq, LANES), F32),
            jax.ShapeDtypeStruct((batch, n_heads, seq), F32),
        ],
        compiler_params=_cparams(("parallel",), 8 * seq * LANES * 4),
        name="dcum",
    )(logf)


def _fox_kernel(q_ref, k_ref, v_ref, dq_ref, dk_ref, o_ref, m_scr, l_scr, acc_scr, *, t):
    h = pl.program_id(1)
    qi = pl.program_id(2)
    q = q_ref[...]
    lane = lax.broadcasted_iota(jnp.int32, (t, LANES), 1)
    dcol = jnp.sum(jnp.where(lane == h, dq_ref[...], 0.0), axis=-1, keepdims=True)

    m_scr[...] = jnp.full((t, 1), NEG_BIG, F32)
    l_scr[...] = jnp.zeros((t, 1), F32)
    acc_scr[...] = jnp.zeros((t, HEAD_DIM), F32)

    def step(ki, masked):
        off = pl.multiple_of(ki * t, t)
        k = k_ref[pl.ds(off, t), :]
        v = v_ref[pl.ds(off, t), :]
        s = lax.dot_general(q, k, (((1,), (1,)), ((), ())), preferred_element_type=F32)
        s = s + (dcol - dk_ref[pl.ds(h, 1), pl.ds(off, t)])
        if masked:
            row = lax.broadcasted_iota(jnp.int32, (t, t), 0)
            col = lax.broadcasted_iota(jnp.int32, (t, t), 1)
            s = jnp.where(col <= row, s, NEG_BIG)
        m_old = m_scr[...]
        m_new = jnp.maximum(m_old, jnp.max(s, axis=-1, keepdims=True))
        p = jnp.exp(s - m_new)
        alpha = jnp.exp(m_old - m_new)
        l_scr[...] = alpha * l_scr[...] + jnp.sum(p, axis=-1, keepdims=True)
        acc_scr[...] = alpha * acc_scr[...] + jnp.dot(p.astype(BF16), v,
                                                      preferred_element_type=F32)
        m_scr[...] = m_new

    def body(ki, c):
        step(ki, False)
        return c

    lax.fori_loop(0, qi, body, 0)
    step(qi, True)
    o_ref[...] = (acc_scr[...] / l_scr[...]).astype(BF16)


def _fox_attn(y, d, dt, *, batch, seq, n_heads, q_col, k_col, v_col, t):
    n = y.shape[0]
    nq = seq // t
    kern = functools.partial(_fox_kernel, t=t)
    vmem = 4 * seq * HEAD_DIM * 2 + 8 * t * t * 4 + 16 * t * LANES * 4 + 2 * 8 * seq * 4
    return pl.pallas_call(
        kern,
        grid=(batch, n_heads, nq),
        in_specs=[
            pl.BlockSpec((t, HEAD_DIM), lambda b, h, i: (b * nq + i, q_col + h)),
            pl.BlockSpec((seq, HEAD_DIM), lambda b, h, i: (b, k_col + h)),
            pl.BlockSpec((seq, HEAD_DIM), lambda b, h, i: (b, v_col + h)),
            pl.BlockSpec((t, LANES), lambda b, h, i: (b * nq + i, 0)),
            pl.BlockSpec((None, n_heads, seq), lambda b, h, i: (b, 0, 0)),
        ],
        out_specs=pl.BlockSpec((t, HEAD_DIM), lambda b, h, i: (b * nq + i, h)),
        out_shape=jax.ShapeDtypeStruct((n, n_heads * HEAD_DIM), BF16),
        scratch_shapes=[pltpu.VMEM((t, 1), F32), pltpu.VMEM((t, 1), F32),
                        pltpu.VMEM((t, HEAD_DIM), F32)],
        compiler_params=_cparams(("parallel", "parallel", "arbitrary"), vmem),
        name="fox_attn",
    )(y, y, y, d, dt)


def _sb_kernel(q_ref, k_ref, v_ref, o_ref, carry_scr, acc_scr, *, t):
    qi = pl.program_id(2)
    q = q_ref[...]
    row = lax.broadcasted_iota(jnp.int32, (t, t), 0)
    col = lax.broadcasted_iota(jnp.int32, (t, t), 1)
    upper = (row > col).astype(BF16)

    carry_scr[...] = jnp.zeros((t, 1), F32)
    acc_scr[...] = jnp.zeros((t, HEAD_DIM), F32)

    def step(ki, masked):
        off = pl.multiple_of(ki * t, t)
        k = k_ref[pl.ds(off, t), :]
        v = v_ref[pl.ds(off, t), :]
        z = lax.dot_general(q, k, (((1,), (1,)), ((), ())), preferred_element_type=F32)
        log_beta = _log_sigmoid(z)
        log_1m = log_beta - z
        if masked:
            keep = col < row
            log_1m = jnp.where(keep, log_1m, 0.0)
        hi = log_1m.astype(BF16)
        lo = (log_1m - hi.astype(F32)).astype(BF16)
        tail = (jnp.dot(hi, upper, preferred_element_type=F32)
                + jnp.dot(lo, upper, preferred_element_type=F32))
        a = jnp.exp(log_beta + tail + carry_scr[...])
        if masked:
            a = jnp.where(keep, a, 0.0)
        acc_scr[...] += jnp.dot(a.astype(BF16), v, preferred_element_type=F32)
        carry_scr[...] += jnp.sum(log_1m, axis=-1, keepdims=True)

    step(qi, True)

    def body(i, c):
        step(qi - 1 - i, False)
        return c

    lax.fori_loop(0, qi, body, 0)
    o_ref[...] = acc_scr[...].astype(BF16)


def _sb_attn(y, *, batch, seq, n_heads, q_col, k_col, v_col, t):
    n = y.shape[0]
    nq = seq // t
    kern = functools.partial(_sb_kernel, t=t)
    vmem = 4 * seq * HEAD_DIM * 2 + 12 * t * t * 4 + 16 * t * LANES * 4
    return pl.pallas_call(
        kern,
        grid=(batch, n_heads, nq),
        in_specs=[
            pl.BlockSpec((t, HEAD_DIM), lambda b, h, i: (b * nq + i, q_col + h)),
            pl.BlockSpec((seq, HEAD_DIM), lambda b, h, i: (b, k_col + h)),
            pl.BlockSpec((seq, HEAD_DIM), lambda b, h, i: (b, v_col + h)),
        ],
        out_specs=pl.BlockSpec((t, HEAD_DIM), lambda b, h, i: (b * nq + i, h)),
        out_shape=jax.ShapeDtypeStruct((n, n_heads * HEAD_DIM), BF16),
        scratch_shapes=[pltpu.VMEM((t, 1), F32), pltpu.VMEM((t, HEAD_DIM), F32)],
        compiler_params=_cparams(("parallel", "parallel", "arbitrary"), vmem),
        name="sb_attn",
    )(y, y, y)


def _merge_kernel(of_ref, os_ref, wf_ref, ws_ref, gf_ref, gs_ref, o_ref):
    a = jnp.dot(of_ref[...], wf_ref[...], preferred_element_type=F32)
    b = jnp.dot(os_ref[...], ws_ref[...], preferred_element_type=F32)
    o_ref[...] = (gf_ref[...].astype(F32) * a + gs_ref[...].astype(F32) * b).astype(BF16)


def _merge(o_fox, o_sb, w_of, w_os, y, *, gate_col, d, tm, tn):
    n, wf = o_fox.shape
    ws = o_sb.shape[1]
    nd = d // tn
    vmem = 2 * tm * (wf + ws) * 2 + 2 * (wf + ws) * tn * 2 + 6 * tm * tn * 2 + 4 * tm * tn * 4
    return pl.pallas_call(
        _merge_kernel,
        grid=(n // tm, nd),
        in_specs=[
            pl.BlockSpec((tm, wf), lambda i, j: (i, 0)),
            pl.BlockSpec((tm, ws), lambda i, j: (i, 0)),
            pl.BlockSpec((wf, tn), lambda i, j: (0, j)),
            pl.BlockSpec((ws, tn), lambda i, j: (0, j)),
            pl.BlockSpec((tm, tn), lambda i, j: (i, gate_col + j)),
            pl.BlockSpec((tm, tn), lambda i, j: (i, gate_col + nd + j)),
        ],
        out_specs=pl.BlockSpec((tm, tn), lambda i, j: (i, j)),
        out_shape=jax.ShapeDtypeStruct((n, d), BF16),
        compiler_params=_cparams(("parallel", "arbitrary"), vmem),
        name="merge",
    )(o_fox, o_sb, w_of, w_os, y, y)


def _out_proj_kernel(m_ref, w_ref, x_ref, o_ref):
    o_ref[...] = x_ref[...] + jnp.dot(m_ref[...], w_ref[...], preferred_element_type=F32)


def _out_proj(merged, w_out, x, *, tm, tn):
    n, d = merged.shape
    vmem = 2 * tm * d * 2 + 2 * d * tn * 2 + 6 * tm * tn * 4
    return pl.pallas_call(
        _out_proj_kernel,
        grid=(n // tm, d // tn),
        in_specs=[
            pl.BlockSpec((tm, d), lambda i, j: (i, 0)),
            pl.BlockSpec((d, tn), lambda i, j: (0, j)),
            pl.BlockSpec((tm, tn), lambda i, j: (i, j)),
        ],
        out_specs=pl.BlockSpec((tm, tn), lambda i, j: (i, j)),
        out_shape=jax.ShapeDtypeStruct((n, d), F32),
        compiler_params=_cparams(("parallel", "arbitrary"), vmem),
        name="out_proj",
    )(merged, w_out, x)


def _ffn_kernel(x_ref, g_ref, w1_ref, w3_ref, w2_ref, o_ref, h_scr):
    f = pl.program_id(1)

    @pl.when(f == 0)
    def _():
        x = x_ref[...]
        h_scr[...] = _rms_rows(x, g_ref[...]).astype(BF16)
        o_ref[...] = x

    h = h_scr[...]
    a = jnp.dot(h, w1_ref[...], preferred_element_type=F32)
    b = jnp.dot(h, w3_ref[...], preferred_element_type=F32)
    mid = (a * (1.0 / (1.0 + jnp.exp(-a))) * b).astype(BF16)
    o_ref[...] += jnp.dot(mid, w2_ref[...], preferred_element_type=F32)


def _ffn(x, g, w1, w3, w2, *, tm, tf):
    n, d = x.shape
    ff = w1.shape[1]
    vmem = 4 * tm * d * 4 + tm * d * 2 + 6 * d * tf * 2 + 5 * tm * tf * 4 + 2 * tm * d * 4
    return pl.pallas_call(
        _ffn_kernel,
        grid=(n // tm, ff // tf),
        in_specs=[
            pl.BlockSpec((tm, d), lambda i, f: (i, 0)),
            pl.BlockSpec((1, d), lambda i, f: (0, 0)),
            pl.BlockSpec((d, tf), lambda i, f: (0, f)),
            pl.BlockSpec((d, tf), lambda i, f: (0, f)),
            pl.BlockSpec((tf, d), lambda i, f: (f, 0)),
        ],
        out_specs=pl.BlockSpec((tm, d), lambda i, f: (i, 0)),
        out_shape=jax.ShapeDtypeStruct((n, d), F32),
        scratch_shapes=[pltpu.VMEM((tm, d), BF16)],
        compiler_params=_cparams(("parallel", "arbitrary"), vmem),
        name="ffn",
    )(x, g, w1, w3, w2)


def _router_kernel(x_ref, g_ref, whi_ref, wlo_ref, r_ref, *, n_experts):
    h = _rms_rows(x_ref[...], g_ref[...])
    h_hi = h.astype(BF16)
    h_lo = (h - h_hi.astype(F32)).astype(BF16)
    logits = (jnp.dot(h_hi, whi_ref[...], preferred_element_type=F32)
              + jnp.dot(h_hi, wlo_ref[...], preferred_element_type=F32)
              + jnp.dot(h_lo, whi_ref[...], preferred_element_type=F32))
    lane = lax.broadcasted_iota(jnp.int32, logits.shape, 1)
    logits = jnp.where(lane < n_experts, logits, NEG_BIG)
    m1 = jnp.max(logits, axis=-1, keepdims=True)
    i1 = jnp.min(jnp.where(logits == m1, lane, LANES), axis=-1, keepdims=True)
    rest = jnp.where(lane == i1, NEG_BIG, logits)
    m2 = jnp.max(rest, axis=-1, keepdims=True)
    i2 = jnp.min(jnp.where(rest == m2, lane, LANES), axis=-1, keepdims=True)
    e = jnp.exp(m2 - m1)
    w1 = 1.0 / (1.0 + e)
    w2 = e * w1
    out = jnp.where(lane == 0, i1.astype(F32),
          jnp.where(lane == 1, i2.astype(F32),
          jnp.where(lane == 2, w1, jnp.where(lane == 3, w2, 0.0))))
    r_ref[...] = out


def _router(x, g, w_hi, w_lo, *, n_experts, tm):
    n, d = x.shape
    kern = functools.partial(_router_kernel, n_experts=n_experts)
    vmem = 2 * tm * d * 4 + 4 * d * LANES * 2 + 4 * tm * d * 4 + 8 * tm * LANES * 4
    return pl.pallas_call(
        kern,
        grid=(n // tm,),
        in_specs=[
            pl.BlockSpec((tm, d), lambda i: (i, 0)),
            pl.BlockSpec((1, d), lambda i: (0, 0)),
            pl.BlockSpec((d, LANES), lambda i: (0, 0)),
            pl.BlockSpec((d, LANES), lambda i: (0, 0)),
        ],
        out_specs=pl.BlockSpec((tm, LANES), lambda i: (i, 0)),
        out_shape=jax.ShapeDtypeStruct((n, LANES), F32),
        compiler_params=_cparams(("parallel",), vmem),
        name="router",
    )(x, g, w_hi, w_lo)


def _experts_kernel(te_ref, nv_ref, nu_ref,
                    tok_ref, dst_ref, sw_ref, g_ref, x_hbm, w1_ref, w3_ref, w2_ref,
                    out_hbm, xs_scr, h_scr, acc_scr, gsem, ssem, *, tmx):
    t = pl.program_id(0)
    f = pl.program_id(1)
    nf = pl.num_programs(1)
    used = t < nu_ref[0]

    def row_in(r, tok):
        return pltpu.make_async_copy(x_hbm.at[pl.ds(tok, 1), :],
                                     xs_scr.at[pl.ds(r, 1), :], gsem)

    def row_out(r, dst):
        return pltpu.make_async_copy(acc_scr.at[pl.ds(r, 1), :],
                                     out_hbm.at[pl.ds(dst, 1), :], ssem)

    @pl.when(used & (f == 0))
    def _():
        def issue(r, c):
            row_in(r, tok_ref[0, 0, r]).start()
            return c
        lax.fori_loop(0, tmx, issue, 0)

        def wait(r, c):
            row_in(r, 0).wait()
            return c
        lax.fori_loop(0, tmx, wait, 0)
        h_scr[...] = _rms_rows(xs_scr[...], g_ref[...]).astype(BF16)
        acc_scr[...] = jnp.zeros_like(acc_scr)

    @pl.when(used)
    def _():
        h = h_scr[...]
        a = jnp.dot(h, w1_ref[...], preferred_element_type=F32)
        b = jnp.dot(h, w3_ref[...], preferred_element_type=F32)
        mid = (a * (1.0 / (1.0 + jnp.exp(-a))) * b).astype(BF16)
        acc_scr[...] += jnp.dot(mid, w2_ref[...], preferred_element_type=F32)

    @pl.when(used & (f == nf - 1))
    def _():
        acc_scr[...] = acc_scr[...] * sw_ref[...]
        nv = nv_ref[t]

        def issue(r, c):
            row_out(r, dst_ref[0, 0, r]).start()
            return c
        lax.fori_loop(0, nv, issue, 0)

        def wait(r, c):
            row_out(r, 0).wait()
            return c
        lax.fori_loop(0, nv, wait, 0)


def _experts(x, g, tile_e, tile_nv, n_used, slot_tok, slot_dst, slot_w, w1, w3, w2, *, tmx, tf):
    n, d = x.shape
    n_tiles = tile_e.shape[0]
    ff = w1.shape[2]
    nf = ff // tf
    kern = functools.partial(_experts_kernel, tmx=tmx)

    def tclamp(t, nu):
        return jnp.minimum(t, nu[0] - 1)

    def fclamp(t, f, nu):
        return jnp.where(t < nu[0], f, nf - 1)

    grid_spec = pltpu.PrefetchScalarGridSpec(
        num_scalar_prefetch=3,
        grid=(n_tiles, nf),
        in_specs=[
            pl.BlockSpec((1, 1, tmx), lambda t, f, te, nv, nu: (tclamp(t, nu), 0, 0),
                         memory_space=pltpu.SMEM),
            pl.BlockSpec((1, 1, tmx), lambda t, f, te, nv, nu: (tclamp(t, nu), 0, 0),
                         memory_space=pltpu.SMEM),
            pl.BlockSpec((tmx, 1), lambda t, f, te, nv, nu: (tclamp(t, nu), 0)),
            pl.BlockSpec((1, d), lambda t, f, te, nv, nu: (0, 0)),
            pl.BlockSpec(memory_space=pl.ANY),
            pl.BlockSpec((None, d, tf),
                         lambda t, f, te, nv, nu: (te[tclamp(t, nu)], 0, fclamp(t, f, nu))),
            pl.BlockSpec((None, d, tf),
                         lambda t, f, te, nv, nu: (te[tclamp(t, nu)], 0, fclamp(t, f, nu))),
            pl.BlockSpec((None, tf, d),
                         lambda t, f, te, nv, nu: (te[tclamp(t, nu)], fclamp(t, f, nu), 0)),
        ],
        out_specs=pl.BlockSpec(memory_space=pl.ANY),
        scratch_shapes=[
            pltpu.VMEM((tmx, d), F32),
            pltpu.VMEM((tmx, d), BF16),
            pltpu.VMEM((tmx, d), F32),
            pltpu.SemaphoreType.DMA(()),
            pltpu.SemaphoreType.DMA(()),
        ],
    )
    vmem = (2 * tmx * d * 4 + tmx * d * 2 + 6 * d * tf * 2 + 5 * tmx * tf * 4
            + 2 * tmx * LANES * 4 + 2 * tmx * d * 4)
    return pl.pallas_call(
        kern,
        grid_spec=grid_spec,
        out_shape=jax.ShapeDtypeStruct((TOP_K * n, d), F32),
        compiler_params=_cparams(("arbitrary", "arbitrary"), vmem),
        name="experts",
    )(tile_e, tile_nv, n_used, slot_tok, slot_dst, slot_w, g, x, w1, w3, w2)


def _combine_kernel(x_ref, y_ref, o_ref, *, d):
    acc = x_ref[...]
    for k in range(TOP_K):
        acc = acc + y_ref[:, k * d:(k + 1) * d]
    o_ref[...] = acc


def _combine(x, y2, *, tm):
    n, d = x.shape
    kern = functools.partial(_combine_kernel, d=d)
    return pl.pallas_call(
        kern,
        grid=(n // tm,),
        in_specs=[
            pl.BlockSpec((tm, d), lambda i: (i, 0)),
            pl.BlockSpec((tm, TOP_K * d), lambda i: (i, 0)),
        ],
        out_specs=pl.BlockSpec((tm, d), lambda i: (i, 0)),
        out_shape=jax.ShapeDtypeStruct((n, d), F32),
        compiler_params=_cparams(("parallel",), 10 * tm * d * 4),
        name="combine",
    )(x, y2)


def _moe_plan(route, *, n_experts, tmx, n_tiles):
    n = route.shape[0]
    idx = route[:, :TOP_K].astype(jnp.int32)
    wts = route[:, TOP_K:2 * TOP_K]
    flat_e = idx.reshape(-1)
    onehot = (flat_e[:, None] == jnp.arange(n_experts, dtype=jnp.int32)[None, :]).astype(jnp.int32)
    ranks = jnp.cumsum(onehot, axis=0) - onehot
    rank = jnp.sum(ranks * onehot, axis=1)
    counts = jnp.sum(onehot, axis=0)
    padded = (counts + tmx - 1) // tmx * tmx
    pad_ends = jnp.cumsum(padded)
    pad_starts = pad_ends - padded
    dest = pad_starts[flat_e] + rank
    n_slots = n_tiles * tmx
    a_ids = jnp.arange(n * TOP_K, dtype=jnp.int32)
    slot_tok = jnp.zeros((n_slots,), jnp.int32).at[dest].set(a_ids // TOP_K)
    slot_dst = jnp.zeros((n_slots,), jnp.int32).at[dest].set(a_ids)
    slot_w = jnp.zeros((n_slots,), F32).at[dest].set(wts.reshape(-1))
    tile_start = jnp.arange(n_tiles, dtype=jnp.int32) * tmx
    tile_e = jnp.minimum(jnp.searchsorted(pad_ends, tile_start, side='right'),
                         n_experts - 1).astype(jnp.int32)
    tile_nv = jnp.clip(counts[tile_e] - (tile_start - pad_starts[tile_e]), 0, tmx).astype(jnp.int32)
    n_used = (pad_ends[-1] // tmx).astype(jnp.int32).reshape(1)
    return (tile_e, tile_nv, n_used, slot_tok.reshape(n_tiles, 1, tmx),
            slot_dst.reshape(n_tiles, 1, tmx), slot_w.reshape(n_slots, 1))


def kernel(x, norm1_g, w_in, qn_g, kn_g, b_f, w_o_fox, w_o_sb, w_gate, b_gate, w_out, norm2_g,
           ffn_w1, ffn_w3, ffn_w2, w_router, moe_w1, moe_w3, moe_w2):
    batch, seq, d = x.shape
    depth = norm1_g.shape[0]
    h_fox = qn_g.shape[1]
    wfox = h_fox * HEAD_DIM
    wsb = (w_in.shape[2] - 3 * wfox - h_fox) // 3
    h_sb = wsb // HEAD_DIM
    n_experts = w_router.shape[2]
    n = batch * seq
    scale = HEAD_DIM ** -0.5

    tn = _pick(wfox, 512)
    assert wsb % tn == 0 and d % tn == 0
    tm = _pick(n, 512)
    t_attn = _pick(seq, 512)
    tf = _pick(ffn_w1.shape[2], 512)
    tfx = _pick(moe_w1.shape[3], 512)
    tmx = _pick(n, 512)
    n_tiles = (n * TOP_K) // tmx + n_experts

    off_f = 3 * wfox
    off_sb = off_f + h_fox
    n_proj = 3 * wfox + 3 * wsb
    hcols = HEAD_DIM

    xf = x.reshape(n, d)
    for layer in range(depth):
        wl = w_in[layer]
        w_cat = jnp.concatenate([wl[:, :off_f], wl[:, off_sb:], w_gate[layer]], axis=1).astype(BF16)
        wf_pad = jnp.zeros((d, LANES), F32).at[:, :h_fox].set(wl[:, off_f:off_sb]).astype(BF16)
        bf_pad = jnp.zeros((1, LANES), F32).at[0, :h_fox].set(b_f[layer])
        cvec = jnp.concatenate([
            qn_g[layer].reshape(-1) * scale, kn_g[layer].reshape(-1),
            jnp.ones((wfox,), F32),
            jnp.full((wsb,), scale, F32), jnp.ones((2 * wsb,), F32),
            jnp.ones((2 * d,), F32)]).reshape(1, -1)
        bvec = jnp.concatenate([jnp.zeros((n_proj,), F32), b_gate[layer]]).reshape(1, -1)

        y, logf = _proj(xf, norm1_g[layer].reshape(1, d), w_cat, wf_pad, cvec, bvec, bf_pad,
                        n_qk_tiles=2 * wfox // tn, n_plain_tiles=(wfox + 3 * wsb) // tn,
                        tm=tm, tn=tn)
        dd, dt = _dcum(logf, batch=batch, seq=seq, n_heads=h_fox)
        o_fox = _fox_attn(y, dd, dt, batch=batch, seq=seq, n_heads=h_fox,
                          q_col=0, k_col=wfox // hcols, v_col=2 * wfox // hcols, t=t_attn)
        sb0 = 3 * wfox // hcols
        o_sb = _sb_attn(y, batch=batch, seq=seq, n_heads=h_sb,
                        q_col=sb0, k_col=sb0 + wsb // hcols, v_col=sb0 + 2 * wsb // hcols,
                        t=t_attn)
        merged = _merge(o_fox, o_sb, w_o_fox[layer].astype(BF16), w_o_sb[layer].astype(BF16), y,
                        gate_col=n_proj // tn, d=d, tm=tm, tn=tn)
        xf = _out_proj(merged, w_out[layer].astype(BF16), xf, tm=tm, tn=tn)

        i = layer // 2
        g2 = norm2_g[layer].reshape(1, d)
        if layer % 2 == 0:
            xf = _ffn(xf, g2, ffn_w1[i].astype(BF16), ffn_w3[i].astype(BF16),
                      ffn_w2[i].astype(BF16), tm=tm, tf=tf)
        else:
            wr = jnp.zeros((d, LANES), F32).at[:, :n_experts].set(w_router[i])
            wr_hi = wr.astype(BF16)
            wr_lo = (wr - wr_hi.astype(F32)).astype(BF16)
            route = _router(xf, g2, wr_hi, wr_lo, n_experts=n_experts, tm=tm)
            plan = _moe_plan(route, n_experts=n_experts, tmx=tmx, n_tiles=n_tiles)
            y2 = _experts(xf, g2, *plan, moe_w1[i].astype(BF16), moe_w3[i].astype(BF16),
                          moe_w2[i].astype(BF16), tmx=tmx, tf=tfx)
            xf = _combine(xf, y2.reshape(n, TOP_K * d), tm=tm)
    return xf.reshape(batch, seq, d)
```

```python
import functools
import math

import jax
import jax.numpy as jnp
from jax import lax
from jax.experimental import pallas as pl
from jax.experimental.pallas import tpu as pltpu

HEAD_DIM = 128
RMS_EPS = 1e-6
TOP_K = 2
LANES = 128
NEG_BIG = -1e30
LOG2E = math.log2(math.e)
SB_DEAD = -120.0
VMEM_LIMIT_CAP = 60000 * 1024
NORM_ROWS = 256

F32 = jnp.float32
BF16 = jnp.bfloat16
NT_DIMS = (((1,), (1,)), ((), ()))


def _cparams(semantics, vmem_bytes):
    limit = int(min(VMEM_LIMIT_CAP, max(32 * 1024 * 1024, vmem_bytes)))
    return pltpu.CompilerParams(dimension_semantics=semantics, vmem_limit_bytes=limit)


def _rms_rows(x, g):
    ms = jnp.mean(x * x, axis=-1, keepdims=True)
    return x * lax.rsqrt(ms + RMS_EPS) * g


def _rms_to(dst_ref, x_ref, g_ref):
    rows = x_ref.shape[0]
    step = min(rows, NORM_ROWS)
    for c in range(rows // step):
        sl = slice(c * step, (c + 1) * step)
        dst_ref[sl, :] = _rms_rows(x_ref[sl, :], g_ref[...]).astype(BF16)


def _log_sigmoid(z):
    return jnp.minimum(z, 0.0) - jnp.log(1.0 + jnp.exp(-jnp.abs(z)))


def _lane_tile(x, reps):
    return x if reps == 1 else jnp.concatenate([x] * reps, axis=1)


def _pick(total, pref):
    t = min(total, pref)
    while total % t:
        t //= 2
    return t


def _proj_kernel(x_ref, g_ref, w_ref, wf_ref, cvec_ref, bvec_ref, bf_ref,
                 y_ref, logf_ref, h_scr, *, n_qk_tiles, n_plain_tiles, tn):
    j = pl.program_id(1)

    @pl.when(j == 0)
    def _():
        _rms_to(h_scr, x_ref, g_ref)
        fl = jnp.dot(h_scr[...], wf_ref[...], preferred_element_type=F32)
        logf_ref[...] = _log_sigmoid(fl + bf_ref[...])

    acc = jnp.dot(h_scr[...], w_ref[...], preferred_element_type=F32)

    @pl.when(j < n_qk_tiles)
    def _():
        for c in range(tn // HEAD_DIM):
            sl = slice(c * HEAD_DIM, (c + 1) * HEAD_DIM)
            y = acc[:, sl]
            ms = jnp.mean(y * y, axis=-1, keepdims=True)
            y_ref[:, sl] = (y * lax.rsqrt(ms + RMS_EPS) * cvec_ref[:, sl]).astype(BF16)

    @pl.when((j >= n_qk_tiles) & (j < n_qk_tiles + n_plain_tiles))
    def _():
        y_ref[...] = (acc * cvec_ref[...]).astype(BF16)

    @pl.when(j >= n_qk_tiles + n_plain_tiles)
    def _():
        z = acc + bvec_ref[...]
        y_ref[...] = (1.0 / (1.0 + jnp.exp(-z))).astype(BF16)


def _proj(x, g, w, wf, cvec, bvec, bf, *, n_qk_tiles, n_plain_tiles, tm, tn):
    n, d = x.shape
    n_out = w.shape[1]
    kern = functools.partial(_proj_kernel, n_qk_tiles=n_qk_tiles,
                             n_plain_tiles=n_plain_tiles, tn=tn)
    vmem = (2 * tm * d * 4 + tm * d * 2 + 2 * d * tn * 2 + 2 * d * LANES * 2
            + 2 * tm * tn * 2 + 2 * tm * LANES * 4 + 6 * tm * tn * 4 + 4 * NORM_ROWS * d * 4)
    return pl.pallas_call(
        kern,
        grid=(n // tm, n_out // tn),
        in_specs=[
            pl.BlockSpec((tm, d), lambda i, j: (i, 0)),
            pl.BlockSpec((1, d), lambda i, j: (0, 0)),
            pl.BlockSpec((d, tn), lambda i, j: (0, j)),
            pl.BlockSpec((d, LANES), lambda i, j: (0, 0)),
            pl.BlockSpec((1, tn), lambda i, j: (0, j)),
            pl.BlockSpec((1, tn), lambda i, j: (0, j)),
            pl.BlockSpec((1, LANES), lambda i, j: (0, 0)),
        ],
        out_specs=[
            pl.BlockSpec((tm, tn), lambda i, j: (i, j)),
            pl.BlockSpec((tm, LANES), lambda i, j: (i, 0)),
        ],
        out_shape=[
            jax.ShapeDtypeStruct((n, n_out), BF16),
            jax.ShapeDtypeStruct((n, LANES), F32),
        ],
        scratch_shapes=[pltpu.VMEM((tm, d), BF16)],
        compiler_params=_cparams(("parallel", "arbitrary"), vmem),
        name="proj",
    )(x, g, w, wf, cvec, bvec, bf)


def _dcum_kernel(lf_ref, aq_ref, ak_ref, *, chunk, n_heads):
    s = lf_ref.shape[0]
    r = lax.broadcasted_iota(jnp.int32, (chunk, chunk), 0)
    c = lax.broadcasted_iota(jnp.int32, (chunk, chunk), 1)
    tri = (c <= r).astype(F32)
    lane = lax.broadcasted_iota(jnp.int32, (chunk, LANES), 1)
    carry = jnp.zeros((1, LANES), F32)
    for i in range(s // chunk):
        sl = slice(i * chunk, (i + 1) * chunk)
        cs = jnp.dot(tri, lf_ref[sl, :], preferred_element_type=F32,
                     precision=lax.Precision.HIGHEST) + carry
        carry = cs[chunk - 1:chunk, :]
        d2 = cs * LOG2E
        for h in range(n_heads):
            col = jnp.broadcast_to(d2[:, h:h + 1], (chunk, LANES))
            hi = col.astype(BF16).astype(F32)
            r1 = col - hi
            mid = r1.astype(BF16).astype(F32)
            lo = (r1 - mid).astype(BF16).astype(F32)
            aq = jnp.where(lane == 0, hi, jnp.where(lane == 1, mid, jnp.where(
                lane == 2, lo, jnp.where(lane < 6, 1.0, 0.0))))
            ak = jnp.where(lane < 3, 1.0, jnp.where(lane == 3, -hi, jnp.where(
                lane == 4, -mid, jnp.where(lane == 5, -lo, 0.0))))
            hs = slice(h * HEAD_DIM, (h + 1) * HEAD_DIM)
            aq_ref[sl, hs] = aq.astype(BF16)
            ak_ref[sl, hs] = ak.astype(BF16)


def _dcum(logf, *, batch, seq, n_heads):
    chunk = _pick(seq, 256)
    kern = functools.partial(_dcum_kernel, chunk=chunk, n_heads=n_heads)
    wid = n_heads * HEAD_DIM
    return pl.pallas_call(
        kern,
        grid=(batch,),
        in_specs=[pl.BlockSpec((seq, LANES), lambda b: (b, 0))],
        out_specs=[
            pl.BlockSpec((seq, wid), lambda b: (b, 0)),
            pl.BlockSpec((seq, wid), lambda b: (b, 0)),
        ],
        out_shape=[
            jax.ShapeDtypeStruct((batch * seq, wid), BF16),
            jax.ShapeDtypeStruct((batch * seq, wid), BF16),
        ],
        compiler_params=_cparams(("parallel",), 4 * seq * LANES * 4 + 4 * seq * wid * 2),
        name="dcum",
    )(logf)


def _fox_kernel(q_ref, aq_ref, k_ref, ak_ref, v_ref, o_ref,
                kaug, vaug, s_scr, m_scr, acc_scr, *, t):
    qi = pl.program_id(2)

    @pl.when(qi == 0)
    def _():
        kaug[:, :HEAD_DIM] = k_ref[...]
        kaug[:, HEAD_DIM:] = ak_ref[...]
        vaug[:, :HEAD_DIM] = v_ref[...]
        vaug[:, HEAD_DIM:] = jnp.ones(v_ref.shape, BF16)

    qa = jnp.concatenate([q_ref[...], aq_ref[...]], axis=1)
    reps = t // LANES

    def logits(ki):
        off = pl.multiple_of(ki * t, t)
        return lax.dot_general(qa, kaug[pl.ds(off, t), :], NT_DIMS, preferred_element_type=F32)

    def update(ki):
        off = pl.multiple_of(ki * t, t)
        s = s_scr[...]
        m_old = m_scr[...]
        m_new = jnp.maximum(m_old, jnp.max(s, axis=-1, keepdims=True))
        alpha = jnp.exp2(m_old - m_new)
        p = jnp.exp2(s - _lane_tile(m_new, reps))
        pv = jnp.dot(p.astype(BF16), vaug[pl.ds(off, t), :], preferred_element_type=F32)
        acc_scr[...] = acc_scr[...] * _lane_tile(alpha, 2) + pv
        m_scr[...] = m_new

    m_scr[...] = jnp.full(m_scr.shape, NEG_BIG, F32)
    acc_scr[...] = jnp.zeros(acc_scr.shape, F32)
    row = lax.broadcasted_iota(jnp.int32, (t, t), 0)
    col = lax.broadcasted_iota(jnp.int32, (t, t), 1)
    s_scr[...] = jnp.where(col <= row, logits(qi), NEG_BIG)

    def body(i, c):
        nxt = logits(qi - 1 - i)
        update(qi - i)
        s_scr[...] = nxt
        return c

    lax.fori_loop(0, qi, body, 0)
    update(0)
    acc = acc_scr[...]
    o_ref[...] = (acc[:, :HEAD_DIM] / acc[:, HEAD_DIM:]).astype(BF16)


def _fox_attn(y, aq, ak, *, batch, seq, n_heads, q_col, k_col, v_col, t):
    n = y.shape[0]
    nq = seq // t
    kern = functools.partial(_fox_kernel, t=t)
    vmem = (6 * seq * HEAD_DIM * 2 + 4 * seq * HEAD_DIM * 2 + 10 * t * t * 4
            + 16 * t * 2 * HEAD_DIM * 4)
    return pl.pallas_call(
        kern,
        grid=(batch, n_heads, nq),
        in_specs=[
            pl.BlockSpec((t, HEAD_DIM), lambda b, h, i: (b * nq + i, q_col + h)),
            pl.BlockSpec((t, HEAD_DIM), lambda b, h, i: (b * nq + i, h)),
            pl.BlockSpec((seq, HEAD_DIM), lambda b, h, i: (b, k_col + h)),
            pl.BlockSpec((seq, HEAD_DIM), lambda b, h, i: (b, h)),
            pl.BlockSpec((seq, HEAD_DIM), lambda b, h, i: (b, v_col + h)),
        ],
        out_specs=pl.BlockSpec((t, HEAD_DIM), lambda b, h, i: (b * nq + i, h)),
        out_shape=jax.ShapeDtypeStruct((n, n_heads * HEAD_DIM), BF16),
        scratch_shapes=[
            pltpu.VMEM((seq, 2 * HEAD_DIM), BF16),
            pltpu.VMEM((seq, 2 * HEAD_DIM), BF16),
            pltpu.VMEM((t, t), F32),
            pltpu.VMEM((t, LANES), F32),
            pltpu.VMEM((t, 2 * HEAD_DIM), F32),
        ],
        compiler_params=_cparams(("parallel", "parallel", "arbitrary"), vmem),
        name="fox_attn",
    )(y, aq, y, ak, y)


def _sb_kernel(q_ref, k_ref, v_ref, o_ref, carry_scr, acc_scr, *, t):
    qi = pl.program_id(2)
    q = q_ref[...]
    row = lax.broadcasted_iota(jnp.int32, (t, t), 0)
    col = lax.broadcasted_iota(jnp.int32, (t, t), 1)
    upper = (row > col).astype(BF16)
    reps = t // LANES

    carry_scr[...] = jnp.zeros(carry_scr.shape, F32)
    acc_scr[...] = jnp.zeros(acc_scr.shape, F32)

    def step(ki, masked):
        off = pl.multiple_of(ki * t, t)
        k = k_ref[pl.ds(off, t), :]
        v = v_ref[pl.ds(off, t), :]
        z = lax.dot_general(q, k, NT_DIMS, preferred_element_type=F32)
        log_beta = _log_sigmoid(z)
        log_1m = log_beta - z
        if masked:
            keep = col < row
            log_1m = jnp.where(keep, log_1m, 0.0)
        hi = log_1m.astype(BF16)
        lo = (log_1m - hi.astype(F32)).astype(BF16)
        tail = (jnp.dot(hi, upper, preferred_element_type=F32)
                + jnp.dot(lo, upper, preferred_element_type=F32))
        carry = carry_scr[...]
        a = jnp.exp(log_beta + tail + _lane_tile(carry, reps))
        if masked:
            a = jnp.where(keep, a, 0.0)
        acc_scr[...] += jnp.dot(a.astype(BF16), v, preferred_element_type=F32)
        carry_scr[...] = carry + jnp.sum(log_1m, axis=-1, keepdims=True)

    step(qi, True)

    def alive(c):
        i, worst = c
        return (i < qi) & (worst > SB_DEAD)

    def body(c):
        i, _ = c
        step(qi - 1 - i, False)
        return i + 1, jnp.max(carry_scr[...])

    lax.while_loop(alive, body, (jnp.int32(0), jnp.max(carry_scr[...])))
    o_ref[...] = acc_scr[...].astype(BF16)


def _sb_attn(y, *, batch, seq, n_heads, q_col, k_col, v_col, t):
    n = y.shape[0]
    nq = seq // t
    kern = functools.partial(_sb_kernel, t=t)
    vmem = 4 * seq * HEAD_DIM * 2 + 14 * t * t * 4 + 16 * t * LANES * 4
    return pl.pallas_call(
        kern,
        grid=(batch, n_heads, nq),
        in_specs=[
            pl.BlockSpec((t, HEAD_DIM), lambda b, h, i: (b * nq + i, q_col + h)),
            pl.BlockSpec((seq, HEAD_DIM), lambda b, h, i: (b, k_col + h)),
            pl.BlockSpec((seq, HEAD_DIM), lambda b, h, i: (b, v_col + h)),
        ],
        out_specs=pl.BlockSpec((t, HEAD_DIM), lambda b, h, i: (b * nq + i, h)),
        out_shape=jax.ShapeDtypeStruct((n, n_heads * HEAD_DIM), BF16),
        scratch_shapes=[pltpu.VMEM((t, LANES), F32), pltpu.VMEM((t, HEAD_DIM), F32)],
        compiler_params=_cparams(("parallel", "parallel", "arbitrary"), vmem),
        name="sb_attn",
    )(y, y, y)


def _merge_kernel(of_ref, os_ref, wf_ref, ws_ref, gf_ref, gs_ref, o_ref):
    a = jnp.dot(of_ref[...], wf_ref[...], preferred_element_type=F32)
    b = jnp.dot(os_ref[...], ws_ref[...], preferred_element_type=F32)
    o_ref[...] = (gf_ref[...].astype(F32) * a + gs_ref[...].astype(F32) * b).astype(BF16)


def _merge(o_fox, o_sb, w_of, w_os, y, *, gate_col, d, tm, tn):
    n, wf = o_fox.shape
    ws = o_sb.shape[1]
    nd = d // tn
    vmem = 2 * tm * (wf + ws) * 2 + 2 * (wf + ws) * tn * 2 + 6 * tm * tn * 2 + 4 * tm * tn * 4
    return pl.pallas_call(
        _merge_kernel,
        grid=(n // tm, nd),
        in_specs=[
            pl.BlockSpec((tm, wf), lambda i, j: (i, 0)),
            pl.BlockSpec((tm, ws), lambda i, j: (i, 0)),
            pl.BlockSpec((wf, tn), lambda i, j: (0, j)),
            pl.BlockSpec((ws, tn), lambda i, j: (0, j)),
            pl.BlockSpec((tm, tn), lambda i, j: (i, gate_col + j)),
            pl.BlockSpec((tm, tn), lambda i, j: (i, gate_col + nd + j)),
        ],
        out_specs=pl.BlockSpec((tm, tn), lambda i, j: (i, j)),
        out_shape=jax.ShapeDtypeStruct((n, d), BF16),
        compiler_params=_cparams(("parallel", "arbitrary"), vmem),
        name="merge",
    )(o_fox, o_sb, w_of, w_os, y, y)


def _out_proj_kernel(m_ref, w_ref, x_ref, o_ref):
    o_ref[...] = x_ref[...] + jnp.dot(m_ref[...], w_ref[...], preferred_element_type=F32)


def _out_proj(merged, w_out, x, *, tm, tn):
    n, d = merged.shape
    vmem = 2 * tm * d * 2 + 2 * d * tn * 2 + 6 * tm * tn * 4
    return pl.pallas_call(
        _out_proj_kernel,
        grid=(n // tm, d // tn),
        in_specs=[
            pl.BlockSpec((tm, d), lambda i, j: (i, 0)),
            pl.BlockSpec((d, tn), lambda i, j: (0, j)),
            pl.BlockSpec((tm, tn), lambda i, j: (i, j)),
        ],
        out_specs=pl.BlockSpec((tm, tn), lambda i, j: (i, j)),
        out_shape=jax.ShapeDtypeStruct((n, d), F32),
        compiler_params=_cparams(("parallel", "arbitrary"), vmem),
        name="out_proj",
    )(merged, w_out, x)


def _swiglu_step(h, w1_ref, w3_ref, w2_ref):
    a = jnp.dot(h, w1_ref[...], preferred_element_type=F32)
    b = jnp.dot(h, w3_ref[...], preferred_element_type=F32)
    mid = (a * (1.0 / (1.0 + jnp.exp(-a))) * b).astype(BF16)
    return jnp.dot(mid, w2_ref[...], preferred_element_type=F32)


def _ffn_kernel(x_ref, g_ref, w1_ref, w3_ref, w2_ref, o_ref, h_scr):
    f = pl.program_id(1)

    @pl.when(f == 0)
    def _():
        _rms_to(h_scr, x_ref, g_ref)
        o_ref[...] = x_ref[...]

    o_ref[...] += _swiglu_step(h_scr[...], w1_ref, w3_ref, w2_ref)


def _ffn(x, g, w1, w3, w2, *, tm, tf):
    n, d = x.shape
    ff = w1.shape[1]
    vmem = 4 * tm * d * 4 + tm * d * 2 + 6 * d * tf * 2 + 5 * tm * tf * 4 + 2 * tm * d * 4
    return pl.pallas_call(
        _ffn_kernel,
        grid=(n // tm, ff // tf),
        in_specs=[
            pl.BlockSpec((tm, d), lambda i, f: (i, 0)),
            pl.BlockSpec((1, d), lambda i, f: (0, 0)),
            pl.BlockSpec((d, tf), lambda i, f: (0, f)),
            pl.BlockSpec((d, tf), lambda i, f: (0, f)),
            pl.BlockSpec((tf, d), lambda i, f: (f, 0)),
        ],
        out_specs=pl.BlockSpec((tm, d), lambda i, f: (i, 0)),
        out_shape=jax.ShapeDtypeStruct((n, d), F32),
        scratch_shapes=[pltpu.VMEM((tm, d), BF16)],
        compiler_params=_cparams(("parallel", "arbitrary"), vmem),
        name="ffn",
    )(x, g, w1, w3, w2)


def _router_kernel(x_ref, g_ref, whi_ref, wlo_ref, r_ref, *, n_experts):
    h = _rms_rows(x_ref[...], g_ref[...])
    h_hi = h.astype(BF16)
    h_lo = (h - h_hi.astype(F32)).astype(BF16)
    logits = (jnp.dot(h_hi, whi_ref[...], preferred_element_type=F32)
              + jnp.dot(h_hi, wlo_ref[...], preferred_element_type=F32)
              + jnp.dot(h_lo, whi_ref[...], preferred_element_type=F32))
    lane = lax.broadcasted_iota(jnp.int32, logits.shape, 1)
    logits = jnp.where(lane < n_experts, logits, NEG_BIG)
    m1 = jnp.max(logits, axis=-1, keepdims=True)
    i1 = jnp.min(jnp.where(logits == m1, lane, LANES), axis=-1, keepdims=True)
    rest = jnp.where(lane == i1, NEG_BIG, logits)
    m2 = jnp.max(rest, axis=-1, keepdims=True)
    i2 = jnp.min(jnp.where(rest == m2, lane, LANES), axis=-1, keepdims=True)
    e = jnp.exp(m2 - m1)
    w1 = 1.0 / (1.0 + e)
    w2 = e * w1
    out = jnp.where(lane == 0, i1.astype(F32),
          jnp.where(lane == 1, i2.astype(F32),
          jnp.where(lane == 2, w1, jnp.where(lane == 3, w2, 0.0))))
    r_ref[...] = out


def _router(x, g, w_hi, w_lo, *, n_experts, tm):
    n, d = x.shape
    kern = functools.partial(_router_kernel, n_experts=n_experts)
    vmem = 2 * tm * d * 4 + 4 * d * LANES * 2 + 4 * tm * d * 4 + 8 * tm * LANES * 4
    return pl.pallas_call(
        kern,
        grid=(n // tm,),
        in_specs=[
            pl.BlockSpec((tm, d), lambda i: (i, 0)),
            pl.BlockSpec((1, d), lambda i: (0, 0)),
            pl.BlockSpec((d, LANES), lambda i: (0, 0)),
            pl.BlockSpec((d, LANES), lambda i: (0, 0)),
        ],
        out_specs=pl.BlockSpec((tm, LANES), lambda i: (i, 0)),
        out_shape=jax.ShapeDtypeStruct((n, LANES), F32),
        compiler_params=_cparams(("parallel",), vmem),
        name="router",
    )(x, g, w_hi, w_lo)


def _experts_kernel(te_ref, nv_ref, nu_ref,
                    tok_ref, tokn_ref, dst_ref, g_ref, x_hbm, w1_ref, w3_ref, w2_ref,
                    out_hbm, xs_scr, h_scr, acc_scr, y_scr, gsem, ssem, *, tmx, d):
    t = pl.program_id(0)
    f = pl.program_id(1)
    nf = pl.num_programs(1)
    nu = nu_ref[0]
    used = t < nu
    buf = t % 2

    def row_in(b, r, tok):
        return pltpu.make_async_copy(x_hbm.at[pl.ds(tok, 1), :],
                                     xs_scr.at[b, pl.ds(r, 1), :], gsem.at[b])

    def row_out(r, dst):
        return pltpu.make_async_copy(
            y_scr.at[pl.ds(r, 1), :],
            out_hbm.at[pl.ds(lax.div(dst, TOP_K), 1),
                       pl.ds(pl.multiple_of(lax.rem(dst, TOP_K) * d, LANES), d)],
            ssem)

    def gather(b, toks_ref):
        def issue(r, c):
            row_in(b, r, toks_ref[0, 0, r]).start()
            return c
        lax.fori_loop(0, tmx, issue, 0, unroll=8)

    def wait_rows(copy_of, count):
        def wait(r, c):
            copy_of(r).wait()
            return c
        if isinstance(count, int):
            lax.fori_loop(0, count, wait, 0, unroll=8)
        else:
            lax.fori_loop(0, count, wait, 0)

    @pl.when(used & (f == 0))
    def _():
        @pl.when(t == 0)
        def _():
            gather(0, tok_ref)
        wait_rows(lambda r: row_in(buf, r, 0), tmx)
        _rms_to(h_scr, xs_scr.at[buf], g_ref)
        acc_scr[...] = jnp.zeros_like(acc_scr)

        @pl.when(t + 1 < nu)
        def _():
            gather(1 - buf, tokn_ref)

    @pl.when(used)
    def _():
        acc_scr[...] += _swiglu_step(h_scr[...], w1_ref, w3_ref, w2_ref)

    @pl.when(used & (f == nf - 1))
    def _():
        @pl.when(t > 0)
        def _():
            wait_rows(lambda r: row_out(r, 0), nv_ref[jnp.maximum(t - 1, 0)])
        y_scr[...] = acc_scr[...]
        nv = nv_ref[t]

        def issue(r, c):
            row_out(r, dst_ref[0, 0, r]).start()
            return c
        lax.fori_loop(0, nv, issue, 0)

        @pl.when(t == nu - 1)
        def _():
            wait_rows(lambda r: row_out(r, 0), nv)


def _experts(x, g, tile_e, tile_nv, n_used, slot_tok, slot_dst, w1, w3, w2, *, tmx, tf):
    n, d = x.shape
    n_tiles = tile_e.shape[0]
    ff = w1.shape[2]
    nf = ff // tf
    kern = functools.partial(_experts_kernel, tmx=tmx, d=d)

    def tclamp(t, nu):
        return jnp.minimum(t, nu[0] - 1)

    def fclamp(t, f, nu):
        return jnp.where(t < nu[0], f, nf - 1)

    def wmap(t, f, te, nv, nu):
        return (te[tclamp(t, nu)], 0, fclamp(t, f, nu))

    grid_spec = pltpu.PrefetchScalarGridSpec(
        num_scalar_prefetch=3,
        grid=(n_tiles, nf),
        in_specs=[
            pl.BlockSpec((1, 1, tmx), lambda t, f, te, nv, nu: (tclamp(t, nu), 0, 0),
                         memory_space=pltpu.SMEM),
            pl.BlockSpec((1, 1, tmx), lambda t, f, te, nv, nu: (tclamp(t + 1, nu), 0, 0),
                         memory_space=pltpu.SMEM),
            pl.BlockSpec((1, 1, tmx), lambda t, f, te, nv, nu: (tclamp(t, nu), 0, 0),
                         memory_space=pltpu.SMEM),
            pl.BlockSpec((1, d), lambda t, f, te, nv, nu: (0, 0)),
            pl.BlockSpec(memory_space=pl.ANY),
            pl.BlockSpec((None, d, tf), wmap),
            pl.BlockSpec((None, d, tf), wmap),
            pl.BlockSpec((None, tf, d),
                         lambda t, f, te, nv, nu: (te[tclamp(t, nu)], fclamp(t, f, nu), 0)),
        ],
        out_specs=pl.BlockSpec(memory_space=pl.ANY),
        scratch_shapes=[
            pltpu.VMEM((2, tmx, d), F32),
            pltpu.VMEM((tmx, d), BF16),
            pltpu.VMEM((tmx, d), F32),
            pltpu.VMEM((tmx, d), F32),
            pltpu.SemaphoreType.DMA((2,)),
            pltpu.SemaphoreType.DMA(()),
        ],
    )
    vmem = (4 * tmx * d * 4 + tmx * d * 2 + 6 * d * tf * 2 + 5 * tmx * tf * 4
            + 2 * tmx * d * 4 + 4 * NORM_ROWS * d * 4)
    return pl.pallas_call(
        kern,
        grid_spec=grid_spec,
        out_shape=jax.ShapeDtypeStruct((n, TOP_K * d), F32),
        compiler_params=_cparams(("arbitrary", "arbitrary"), vmem),
        name="experts",
    )(tile_e, tile_nv, n_used, slot_tok, slot_tok, slot_dst, g, x, w1, w3, w2)


def _combine_kernel(x_ref, y_ref, r_ref, o_ref, *, d):
    acc = x_ref[...]
    for k in range(TOP_K):
        acc = acc + y_ref[:, k * d:(k + 1) * d] * r_ref[:, TOP_K + k:TOP_K + k + 1]
    o_ref[...] = acc


def _combine(x, y2, route, *, tm):
    n, d = x.shape
    kern = functools.partial(_combine_kernel, d=d)
    return pl.pallas_call(
        kern,
        grid=(n // tm,),
        in_specs=[
            pl.BlockSpec((tm, d), lambda i: (i, 0)),
            pl.BlockSpec((tm, TOP_K * d), lambda i: (i, 0)),
            pl.BlockSpec((tm, LANES), lambda i: (i, 0)),
        ],
        out_specs=pl.BlockSpec((tm, d), lambda i: (i, 0)),
        out_shape=jax.ShapeDtypeStruct((n, d), F32),
        compiler_params=_cparams(("parallel",), 12 * tm * d * 4),
        name="combine",
    )(x, y2, route)


def _moe_plan(route, *, n_experts, tmx, n_tiles):
    n = route.shape[0]
    flat_e = route[:, :TOP_K].astype(jnp.int32).reshape(-1)
    onehot = (flat_e[:, None] == jnp.arange(n_experts, dtype=jnp.int32)[None, :]).astype(jnp.int32)
    rank = jnp.sum((jnp.cumsum(onehot, axis=0) - onehot) * onehot, axis=1)
    counts = jnp.sum(onehot, axis=0)
    padded = (counts + tmx - 1) // tmx * tmx
    pad_ends = jnp.cumsum(padded)
    pad_starts = pad_ends - padded
    dest = jnp.sum(onehot * pad_starts[None, :], axis=1) + rank
    slot_a = jnp.zeros((n_tiles * tmx,), jnp.int32).at[dest].set(
        jnp.arange(n * TOP_K, dtype=jnp.int32))
    tile_start = jnp.arange(n_tiles, dtype=jnp.int32) * tmx
    tile_e = jnp.minimum(jnp.sum((tile_start[:, None] >= pad_ends[None, :]).astype(jnp.int32), axis=1),
                         n_experts - 1)
    tile_nv = jnp.clip(counts[tile_e] - (tile_start - pad_starts[tile_e]), 0, tmx).astype(jnp.int32)
    n_used = (pad_ends[-1] // tmx).astype(jnp.int32).reshape(1)
    return (tile_e, tile_nv, n_used, (slot_a // TOP_K).reshape(n_tiles, 1, tmx),
            slot_a.reshape(n_tiles, 1, tmx))


def kernel(x, norm1_g, w_in, qn_g, kn_g, b_f, w_o_fox, w_o_sb, w_gate, b_gate, w_out, norm2_g,
           ffn_w1, ffn_w3, ffn_w2, w_router, moe_w1, moe_w3, moe_w2):
    batch, seq, d = x.shape
    depth = norm1_g.shape[0]
    h_fox = qn_g.shape[1]
    wfox = h_fox * HEAD_DIM
    wsb = (w_in.shape[2] - 3 * wfox - h_fox) // 3
    h_sb = wsb // HEAD_DIM
    n_experts = w_router.shape[2]
    n = batch * seq
    scale = HEAD_DIM ** -0.5

    tn = _pick(wfox, 512)
    assert wsb % tn == 0 and d % tn == 0
    tm = _pick(n, 512)
    tm_proj = _pick(n, 1024)
    t_fox = _pick(seq, 512)
    t_sb = _pick(seq, 256)
    tf = _pick(ffn_w1.shape[2], 512)
    tfx = _pick(moe_w1.shape[3], 512)
    tmx = _pick(n, 512)
    n_tiles = (n * TOP_K) // tmx + n_experts

    off_f = 3 * wfox
    off_sb = off_f + h_fox
    n_proj = 3 * wfox + 3 * wsb
    hcols = HEAD_DIM

    xf = x.reshape(n, d)
    for layer in range(depth):
        wl = w_in[layer]
        w_cat = jnp.concatenate([wl[:, :off_f], wl[:, off_sb:], w_gate[layer]], axis=1).astype(BF16)
        wf_pad = jnp.zeros((d, LANES), F32).at[:, :h_fox].set(wl[:, off_f:off_sb]).astype(BF16)
        bf_pad = jnp.zeros((1, LANES), F32).at[0, :h_fox].set(b_f[layer])
        cvec = jnp.concatenate([
            qn_g[layer].reshape(-1) * (scale * LOG2E), kn_g[layer].reshape(-1),
            jnp.ones((wfox,), F32),
            jnp.full((wsb,), scale, F32), jnp.ones((2 * wsb,), F32),
            jnp.ones((2 * d,), F32)]).reshape(1, -1)
        bvec = jnp.concatenate([jnp.zeros((n_proj,), F32), b_gate[layer]]).reshape(1, -1)

        y, logf = _proj(xf, norm1_g[layer].reshape(1, d), w_cat, wf_pad, cvec, bvec, bf_pad,
                        n_qk_tiles=2 * wfox // tn, n_plain_tiles=(wfox + 3 * wsb) // tn,
                        tm=tm_proj, tn=tn)
        aq, ak = _dcum(logf, batch=batch, seq=seq, n_heads=h_fox)
        o_fox = _fox_attn(y, aq, ak, batch=batch, seq=seq, n_heads=h_fox,
                          q_col=0, k_col=wfox // hcols, v_col=2 * wfox // hcols, t=t_fox)
        sb0 = 3 * wfox // hcols
        o_sb = _sb_attn(y, batch=batch, seq=seq, n_heads=h_sb,
                        q_col=sb0, k_col=sb0 + wsb // hcols, v_col=sb0 + 2 * wsb // hcols,
                        t=t_sb)
        merged = _merge(o_fox, o_sb, w_o_fox[layer].astype(BF16), w_o_sb[layer].astype(BF16), y,
                        gate_col=n_proj // tn, d=d, tm=tm_proj, tn=tn)
        xf = _out_proj(merged, w_out[layer].astype(BF16), xf, tm=tm_proj, tn=tn)

        i = layer // 2
        g2 = norm2_g[layer].reshape(1, d)
        if layer % 2 == 0:
            xf = _ffn(xf, g2, ffn_w1[i].astype(BF16), ffn_w3[i].astype(BF16),
                      ffn_w2[i].astype(BF16), tm=tm, tf=tf)
        else:
            wr = jnp.zeros((d, LANES), F32).at[:, :n_experts].set(w_router[i])
            wr_hi = wr.astype(BF16)
            wr_lo = (wr - wr_hi.astype(F32)).astype(BF16)
            route = _router(xf, g2, wr_hi, wr_lo, n_experts=n_experts, tm=tm)
            plan = _moe_plan(route, n_experts=n_experts, tmx=tmx, n_tiles=n_tiles)
            y2 = _experts(xf, g2, *plan, moe_w1[i].astype(BF16), moe_w3[i].astype(BF16),
                          moe_w2[i].astype(BF16), tmx=tmx, tf=tfx)
            xf = _combine(xf, y2, route, tm=tm)
    return xf.reshape(batch, seq, d)
```

```python
import functools
import math

import jax
import jax.numpy as jnp
from jax import lax
from jax.experimental import pallas as pl
from jax.experimental.pallas import tpu as pltpu

HEAD_DIM = 128
RMS_EPS = 1e-6
TOP_K = 2
LANES = 128
NEG_BIG = -1e30
LOG2E = math.log2(math.e)
SB_DEAD = -120.0
VMEM_LIMIT_CAP = 60000 * 1024
NORM_ROWS = 256

F32 = jnp.float32
BF16 = jnp.bfloat16
NT_DIMS = (((1,), (1,)), ((), ()))


def _cparams(semantics, vmem_bytes):
    limit = int(min(VMEM_LIMIT_CAP, max(32 * 1024 * 1024, vmem_bytes)))
    return pltpu.CompilerParams(dimension_semantics=semantics, vmem_limit_bytes=limit)


def _rms_rows(x, g):
    ms = jnp.mean(x * x, axis=-1, keepdims=True)
    return x * lax.rsqrt(ms + RMS_EPS) * g


def _rms_to(dst_ref, x_ref, g_ref):
    rows = x_ref.shape[0]
    step = min(rows, NORM_ROWS)
    for c in range(rows // step):
        sl = slice(c * step, (c + 1) * step)
        dst_ref[sl, :] = _rms_rows(x_ref[sl, :], g_ref[...]).astype(BF16)


def _log_sigmoid(z):
    return jnp.minimum(z, 0.0) - jnp.log(1.0 + jnp.exp(-jnp.abs(z)))


def _lane_tile(x, reps):
    return x if reps == 1 else jnp.concatenate([x] * reps, axis=1)


def _pick(total, pref):
    t = min(total, pref)
    while total % t:
        t //= 2
    return t


def _proj_kernel(x_ref, g_ref, w_ref, wf_ref, cvec_ref, bvec_ref, bf_ref,
                 y_ref, logf_ref, h_scr, *, n_qk_tiles, n_plain_tiles, tn):
    j = pl.program_id(1)

    @pl.when(j == 0)
    def _():
        _rms_to(h_scr, x_ref, g_ref)
        fl = jnp.dot(h_scr[...], wf_ref[...], preferred_element_type=F32)
        logf_ref[...] = _log_sigmoid(fl + bf_ref[...])

    acc = jnp.dot(h_scr[...], w_ref[...], preferred_element_type=F32)

    @pl.when(j < n_qk_tiles)
    def _():
        for c in range(tn // HEAD_DIM):
            sl = slice(c * HEAD_DIM, (c + 1) * HEAD_DIM)
            y = acc[:, sl]
            ms = jnp.mean(y * y, axis=-1, keepdims=True)
            y_ref[:, sl] = (y * lax.rsqrt(ms + RMS_EPS) * cvec_ref[:, sl]).astype(BF16)

    @pl.when((j >= n_qk_tiles) & (j < n_qk_tiles + n_plain_tiles))
    def _():
        y_ref[...] = (acc * cvec_ref[...]).astype(BF16)

    @pl.when(j >= n_qk_tiles + n_plain_tiles)
    def _():
        z = acc + bvec_ref[...]
        y_ref[...] = (1.0 / (1.0 + jnp.exp(-z))).astype(BF16)


def _proj(x, g, w, wf, cvec, bvec, bf, *, n_qk_tiles, n_plain_tiles, tm, tn):
    n, d = x.shape
    n_out = w.shape[1]
    kern = functools.partial(_proj_kernel, n_qk_tiles=n_qk_tiles,
                             n_plain_tiles=n_plain_tiles, tn=tn)
    vmem = (2 * tm * d * 4 + tm * d * 2 + 2 * d * tn * 2 + 2 * d * LANES * 2
            + 2 * tm * tn * 2 + 2 * tm * LANES * 4 + 6 * tm * tn * 4 + 4 * NORM_ROWS * d * 4)
    return pl.pallas_call(
        kern,
        grid=(n // tm, n_out // tn),
        in_specs=[
            pl.BlockSpec((tm, d), lambda i, j: (i, 0)),
            pl.BlockSpec((1, d), lambda i, j: (0, 0)),
            pl.BlockSpec((d, tn), lambda i, j: (0, j)),
            pl.BlockSpec((d, LANES), lambda i, j: (0, 0)),
            pl.BlockSpec((1, tn), lambda i, j: (0, j)),
            pl.BlockSpec((1, tn), lambda i, j: (0, j)),
            pl.BlockSpec((1, LANES), lambda i, j: (0, 0)),
        ],
        out_specs=[
            pl.BlockSpec((tm, tn), lambda i, j: (i, j)),
            pl.BlockSpec((tm, LANES), lambda i, j: (i, 0)),
        ],
        out_shape=[
            jax.ShapeDtypeStruct((n, n_out), BF16),
            jax.ShapeDtypeStruct((n, LANES), F32),
        ],
        scratch_shapes=[pltpu.VMEM((tm, d), BF16)],
        compiler_params=_cparams(("parallel", "arbitrary"), vmem),
        name="proj",
    )(x, g, w, wf, cvec, bvec, bf)


def _dcum_kernel(lf_ref, aq_ref, ak_ref, *, chunk, n_heads):
    s = lf_ref.shape[0]
    r = lax.broadcasted_iota(jnp.int32, (chunk, chunk), 0)
    c = lax.broadcasted_iota(jnp.int32, (chunk, chunk), 1)
    tri = (c <= r).astype(F32)
    lane = lax.broadcasted_iota(jnp.int32, (chunk, LANES), 1)
    carry = jnp.zeros((1, LANES), F32)
    for i in range(s // chunk):
        sl = slice(i * chunk, (i + 1) * chunk)
        cs = jnp.dot(tri, lf_ref[sl, :], preferred_element_type=F32,
                     precision=lax.Precision.HIGHEST) + carry
        carry = cs[chunk - 1:chunk, :]
        d2 = cs * LOG2E
        for h in range(n_heads):
            col = jnp.broadcast_to(d2[:, h:h + 1], (chunk, LANES))
            hi = col.astype(BF16).astype(F32)
            r1 = col - hi
            mid = r1.astype(BF16).astype(F32)
            lo = (r1 - mid).astype(BF16).astype(F32)
            aq = jnp.where(lane == 0, hi, jnp.where(lane == 1, mid, jnp.where(
                lane == 2, lo, jnp.where(lane < 6, 1.0, 0.0))))
            ak = jnp.where(lane < 3, 1.0, jnp.where(lane == 3, -hi, jnp.where(
                lane == 4, -mid, jnp.where(lane == 5, -lo, 0.0))))
            hs = slice(h * HEAD_DIM, (h + 1) * HEAD_DIM)
            aq_ref[sl, hs] = aq.astype(BF16)
            ak_ref[sl, hs] = ak.astype(BF16)


def _dcum(logf, *, batch, seq, n_heads):
    chunk = _pick(seq, 256)
    kern = functools.partial(_dcum_kernel, chunk=chunk, n_heads=n_heads)
    wid = n_heads * HEAD_DIM
    return pl.pallas_call(
        kern,
        grid=(batch,),
        in_specs=[pl.BlockSpec((seq, LANES), lambda b: (b, 0))],
        out_specs=[
            pl.BlockSpec((seq, wid), lambda b: (b, 0)),
            pl.BlockSpec((seq, wid), lambda b: (b, 0)),
        ],
        out_shape=[
            jax.ShapeDtypeStruct((batch * seq, wid), BF16),
            jax.ShapeDtypeStruct((batch * seq, wid), BF16),
        ],
        compiler_params=_cparams(("parallel",), 4 * seq * LANES * 4 + 4 * seq * wid * 2),
        name="dcum",
    )(logf)


def _fox_kernel(q_ref, aq_ref, k_ref, ak_ref, v_ref, o_ref,
                kaug, vaug, s_scr, m_scr, acc_scr, *, t, hg):
    qi = pl.program_id(2)
    hd2 = 2 * HEAD_DIM

    def hcols(g):
        return slice(g * HEAD_DIM, (g + 1) * HEAD_DIM)

    @pl.when(qi == 0)
    def _():
        for g in range(hg):
            kaug[:, g * hd2:g * hd2 + HEAD_DIM] = k_ref[:, hcols(g)]
            kaug[:, g * hd2 + HEAD_DIM:(g + 1) * hd2] = ak_ref[:, hcols(g)]
            vaug[:, g * hd2:g * hd2 + HEAD_DIM] = v_ref[:, hcols(g)]
            vaug[:, g * hd2 + HEAD_DIM:(g + 1) * hd2] = jnp.ones((v_ref.shape[0], HEAD_DIM), BF16)

    qa = [jnp.concatenate([q_ref[:, hcols(g)], aq_ref[:, hcols(g)]], axis=1) for g in range(hg)]
    reps = t // LANES

    def logits(ki):
        off = pl.multiple_of(ki * t, t)
        return [lax.dot_general(qa[g], kaug[pl.ds(off, t), g * hd2:(g + 1) * hd2], NT_DIMS,
                                preferred_element_type=F32) for g in range(hg)]

    def update(ki):
        off = pl.multiple_of(ki * t, t)
        for g in range(hg):
            s = s_scr[g]
            m_old = m_scr[g]
            m_new = jnp.maximum(m_old, jnp.max(s, axis=-1, keepdims=True))
            alpha = jnp.exp2(m_old - m_new)
            p = jnp.exp2(s - _lane_tile(m_new, reps))
            pv = jnp.dot(p.astype(BF16), vaug[pl.ds(off, t), g * hd2:(g + 1) * hd2],
                         preferred_element_type=F32)
            acc_scr[g] = acc_scr[g] * _lane_tile(alpha, 2) + pv
            m_scr[g] = m_new

    m_scr[...] = jnp.full(m_scr.shape, NEG_BIG, F32)
    acc_scr[...] = jnp.zeros(acc_scr.shape, F32)
    row = lax.broadcasted_iota(jnp.int32, (t, t), 0)
    col = lax.broadcasted_iota(jnp.int32, (t, t), 1)
    for g, s in enumerate(logits(qi)):
        s_scr[g] = jnp.where(col <= row, s, NEG_BIG)

    def body(i, c):
        nxt = logits(qi - 1 - i)
        update(qi - i)
        for g in range(hg):
            s_scr[g] = nxt[g]
        return c

    lax.fori_loop(0, qi, body, 0)
    update(0)
    for g in range(hg):
        acc = acc_scr[g]
        o_ref[:, hcols(g)] = (acc[:, :HEAD_DIM] / acc[:, HEAD_DIM:]).astype(BF16)


def _fox_attn(y, aq, ak, *, batch, seq, n_heads, q_col, k_col, v_col, t, hg):
    n = y.shape[0]
    nq = seq // t
    wid = hg * HEAD_DIM
    kern = functools.partial(_fox_kernel, t=t, hg=hg)
    vmem = (10 * seq * wid * 2 + 4 * seq * wid * 2 + hg * 10 * t * t * 4
            + hg * 16 * t * 2 * HEAD_DIM * 4)
    return pl.pallas_call(
        kern,
        grid=(batch, n_heads // hg, nq),
        in_specs=[
            pl.BlockSpec((t, wid), lambda b, h, i: (b * nq + i, q_col + h)),
            pl.BlockSpec((t, wid), lambda b, h, i: (b * nq + i, h)),
            pl.BlockSpec((seq, wid), lambda b, h, i: (b, k_col + h)),
            pl.BlockSpec((seq, wid), lambda b, h, i: (b, h)),
            pl.BlockSpec((seq, wid), lambda b, h, i: (b, v_col + h)),
        ],
        out_specs=pl.BlockSpec((t, wid), lambda b, h, i: (b * nq + i, h)),
        out_shape=jax.ShapeDtypeStruct((n, n_heads * HEAD_DIM), BF16),
        scratch_shapes=[
            pltpu.VMEM((seq, 2 * wid), BF16),
            pltpu.VMEM((seq, 2 * wid), BF16),
            pltpu.VMEM((hg, t, t), F32),
            pltpu.VMEM((hg, t, LANES), F32),
            pltpu.VMEM((hg, t, 2 * HEAD_DIM), F32),
        ],
        compiler_params=_cparams(("parallel", "parallel", "arbitrary"), vmem),
        name="fox_attn",
    )(y, aq, y, ak, y)


def _sb_kernel(q_ref, k_ref, v_ref, o_ref, carry_scr, acc_scr, *, t, hg):
    qi = pl.program_id(2)
    row = lax.broadcasted_iota(jnp.int32, (t, t), 0)
    col = lax.broadcasted_iota(jnp.int32, (t, t), 1)
    upper = (row > col).astype(BF16)
    reps = t // LANES

    def hcols(g):
        return slice(g * HEAD_DIM, (g + 1) * HEAD_DIM)

    carry_scr[...] = jnp.zeros(carry_scr.shape, F32)
    acc_scr[...] = jnp.zeros(acc_scr.shape, F32)

    def step(ki, masked):
        off = pl.multiple_of(ki * t, t)
        for g in range(hg):
            k = k_ref[pl.ds(off, t), hcols(g)]
            v = v_ref[pl.ds(off, t), hcols(g)]
            z = lax.dot_general(q_ref[:, hcols(g)], k, NT_DIMS, preferred_element_type=F32)
            log_beta = _log_sigmoid(z)
            log_1m = log_beta - z
            if masked:
                keep = col < row
                log_1m = jnp.where(keep, log_1m, 0.0)
            hi = log_1m.astype(BF16)
            lo = (log_1m - hi.astype(F32)).astype(BF16)
            tail = (jnp.dot(hi, upper, preferred_element_type=F32)
                    + jnp.dot(lo, upper, preferred_element_type=F32))
            carry = carry_scr[g]
            a = jnp.exp(log_beta + tail + _lane_tile(carry, reps))
            if masked:
                a = jnp.where(keep, a, 0.0)
            acc_scr[g] += jnp.dot(a.astype(BF16), v, preferred_element_type=F32)
            carry_scr[g] = carry + jnp.sum(log_1m, axis=-1, keepdims=True)

    step(qi, True)

    def alive(c):
        i, worst = c
        return (i < qi) & (worst > SB_DEAD)

    def body(c):
        i, _ = c
        step(qi - 1 - i, False)
        return i + 1, jnp.max(carry_scr[...])

    lax.while_loop(alive, body, (jnp.int32(0), jnp.max(carry_scr[...])))
    for g in range(hg):
        o_ref[:, hcols(g)] = acc_scr[g].astype(BF16)


def _sb_attn(y, *, batch, seq, n_heads, q_col, k_col, v_col, t, hg):
    n = y.shape[0]
    nq = seq // t
    wid = hg * HEAD_DIM
    kern = functools.partial(_sb_kernel, t=t, hg=hg)
    vmem = 4 * seq * wid * 2 + hg * 14 * t * t * 4 + hg * 16 * t * LANES * 4
    return pl.pallas_call(
        kern,
        grid=(batch, n_heads // hg, nq),
        in_specs=[
            pl.BlockSpec((t, wid), lambda b, h, i: (b * nq + i, q_col + h)),
            pl.BlockSpec((seq, wid), lambda b, h, i: (b, k_col + h)),
            pl.BlockSpec((seq, wid), lambda b, h, i: (b, v_col + h)),
        ],
        out_specs=pl.BlockSpec((t, wid), lambda b, h, i: (b * nq + i, h)),
        out_shape=jax.ShapeDtypeStruct((n, n_heads * HEAD_DIM), BF16),
        scratch_shapes=[pltpu.VMEM((hg, t, LANES), F32), pltpu.VMEM((hg, t, HEAD_DIM), F32)],
        compiler_params=_cparams(("parallel", "parallel", "arbitrary"), vmem),
        name="sb_attn",
    )(y, y, y)


def _merge_kernel(of_ref, os_ref, wf_ref, ws_ref, gf_ref, gs_ref, o_ref):
    a = jnp.dot(of_ref[...], wf_ref[...], preferred_element_type=F32)
    b = jnp.dot(os_ref[...], ws_ref[...], preferred_element_type=F32)
    o_ref[...] = (gf_ref[...].astype(F32) * a + gs_ref[...].astype(F32) * b).astype(BF16)


def _merge(o_fox, o_sb, w_of, w_os, y, *, gate_col, d, tm, tn):
    n, wf = o_fox.shape
    ws = o_sb.shape[1]
    nd = d // tn
    vmem = 2 * tm * (wf + ws) * 2 + 2 * (wf + ws) * tn * 2 + 6 * tm * tn * 2 + 4 * tm * tn * 4
    return pl.pallas_call(
        _merge_kernel,
        grid=(n // tm, nd),
        in_specs=[
            pl.BlockSpec((tm, wf), lambda i, j: (i, 0)),
            pl.BlockSpec((tm, ws), lambda i, j: (i, 0)),
            pl.BlockSpec((wf, tn), lambda i, j: (0, j)),
            pl.BlockSpec((ws, tn), lambda i, j: (0, j)),
            pl.BlockSpec((tm, tn), lambda i, j: (i, gate_col + j)),
            pl.BlockSpec((tm, tn), lambda i, j: (i, gate_col + nd + j)),
        ],
        out_specs=pl.BlockSpec((tm, tn), lambda i, j: (i, j)),
        out_shape=jax.ShapeDtypeStruct((n, d), BF16),
        compiler_params=_cparams(("parallel", "arbitrary"), vmem),
        name="merge",
    )(o_fox, o_sb, w_of, w_os, y, y)


def _out_proj_kernel(m_ref, w_ref, x_ref, o_ref):
    o_ref[...] = x_ref[...] + jnp.dot(m_ref[...], w_ref[...], preferred_element_type=F32)


def _out_proj(merged, w_out, x, *, tm, tn):
    n, d = merged.shape
    vmem = 2 * tm * d * 2 + 2 * d * tn * 2 + 6 * tm * tn * 4
    return pl.pallas_call(
        _out_proj_kernel,
        grid=(n // tm, d // tn),
        in_specs=[
            pl.BlockSpec((tm, d), lambda i, j: (i, 0)),
            pl.BlockSpec((d, tn), lambda i, j: (0, j)),
            pl.BlockSpec((tm, tn), lambda i, j: (i, j)),
        ],
        out_specs=pl.BlockSpec((tm, tn), lambda i, j: (i, j)),
        out_shape=jax.ShapeDtypeStruct((n, d), F32),
        compiler_params=_cparams(("parallel", "arbitrary"), vmem),
        name="out_proj",
    )(merged, w_out, x)


def _swiglu_step(h, w1_ref, w3_ref, w2_ref):
    a = jnp.dot(h, w1_ref[...], preferred_element_type=F32)
    b = jnp.dot(h, w3_ref[...], preferred_element_type=F32)
    mid = (a * (1.0 / (1.0 + jnp.exp(-a))) * b).astype(BF16)
    return jnp.dot(mid, w2_ref[...], preferred_element_type=F32)


def _ffn_kernel(x_ref, g_ref, w1_ref, w3_ref, w2_ref, o_ref, h_scr):
    f = pl.program_id(1)

    @pl.when(f == 0)
    def _():
        _rms_to(h_scr, x_ref, g_ref)
        o_ref[...] = x_ref[...]

    o_ref[...] += _swiglu_step(h_scr[...], w1_ref, w3_ref, w2_ref)


def _ffn(x, g, w1, w3, w2, *, tm, tf):
    n, d = x.shape
    ff = w1.shape[1]
    vmem = 4 * tm * d * 4 + tm * d * 2 + 6 * d * tf * 2 + 5 * tm * tf * 4 + 2 * tm * d * 4
    return pl.pallas_call(
        _ffn_kernel,
        grid=(n // tm, ff // tf),
        in_specs=[
            pl.BlockSpec((tm, d), lambda i, f: (i, 0)),
            pl.BlockSpec((1, d), lambda i, f: (0, 0)),
            pl.BlockSpec((d, tf), lambda i, f: (0, f)),
            pl.BlockSpec((d, tf), lambda i, f: (0, f)),
            pl.BlockSpec((tf, d), lambda i, f: (f, 0)),
        ],
        out_specs=pl.BlockSpec((tm, d), lambda i, f: (i, 0)),
        out_shape=jax.ShapeDtypeStruct((n, d), F32),
        scratch_shapes=[pltpu.VMEM((tm, d), BF16)],
        compiler_params=_cparams(("parallel", "arbitrary"), vmem),
        name="ffn",
    )(x, g, w1, w3, w2)


def _router_kernel(x_ref, g_ref, whi_ref, wlo_ref, r_ref, *, n_experts):
    h = _rms_rows(x_ref[...], g_ref[...])
    h_hi = h.astype(BF16)
    h_lo = (h - h_hi.astype(F32)).astype(BF16)
    logits = (jnp.dot(h_hi, whi_ref[...], preferred_element_type=F32)
              + jnp.dot(h_hi, wlo_ref[...], preferred_element_type=F32)
              + jnp.dot(h_lo, whi_ref[...], preferred_element_type=F32))
    lane = lax.broadcasted_iota(jnp.int32, logits.shape, 1)
    logits = jnp.where(lane < n_experts, logits, NEG_BIG)
    m1 = jnp.max(logits, axis=-1, keepdims=True)
    i1 = jnp.min(jnp.where(logits == m1, lane, LANES), axis=-1, keepdims=True)
    rest = jnp.where(lane == i1, NEG_BIG, logits)
    m2 = jnp.max(rest, axis=-1, keepdims=True)
    i2 = jnp.min(jnp.where(rest == m2, lane, LANES), axis=-1, keepdims=True)
    e = jnp.exp(m2 - m1)
    w1 = 1.0 / (1.0 + e)
    w2 = e * w1
    out = jnp.where(lane == 0, i1.astype(F32),
          jnp.where(lane == 1, i2.astype(F32),
          jnp.where(lane == 2, w1, jnp.where(lane == 3, w2, 0.0))))
    r_ref[...] = out


def _router(x, g, w_hi, w_lo, *, n_experts, tm):
    n, d = x.shape
    kern = functools.partial(_router_kernel, n_experts=n_experts)
    vmem = 2 * tm * d * 4 + 4 * d * LANES * 2 + 4 * tm * d * 4 + 8 * tm * LANES * 4
    return pl.pallas_call(
        kern,
        grid=(n // tm,),
        in_specs=[
            pl.BlockSpec((tm, d), lambda i: (i, 0)),
            pl.BlockSpec((1, d), lambda i: (0, 0)),
            pl.BlockSpec((d, LANES), lambda i: (0, 0)),
            pl.BlockSpec((d, LANES), lambda i: (0, 0)),
        ],
        out_specs=pl.BlockSpec((tm, LANES), lambda i: (i, 0)),
        out_shape=jax.ShapeDtypeStruct((n, LANES), F32),
        compiler_params=_cparams(("parallel",), vmem),
        name="router",
    )(x, g, w_hi, w_lo)


def _experts_kernel(te_ref, nv_ref, nu_ref,
                    tok_ref, tokn_ref, dst_ref, g_ref, x_hbm, w1_ref, w3_ref, w2_ref,
                    out_hbm, xs_scr, h_scr, acc_scr, y_scr, gsem, ssem, *, tmx, sub):
    t = pl.program_id(0)
    f = pl.program_id(1)
    nf = pl.num_programs(1)
    nu = nu_ref[0]
    used = t < nu
    nv = nv_ref[t]

    def row_in(r, tok):
        return pltpu.make_async_copy(x_hbm.at[pl.ds(tok, 1), :], xs_scr.at[pl.ds(r, 1), :], gsem)

    def row_out(r, dst):
        return pltpu.make_async_copy(y_scr.at[pl.ds(r, 1), :], out_hbm.at[pl.ds(dst, 1), :], ssem)

    def gather(toks_ref):
        def issue(r, c):
            row_in(r, toks_ref[0, 0, r]).start()
            return c
        lax.fori_loop(0, tmx, issue, 0, unroll=8)

    def wait_rows(copy_of, count):
        def wait(r, c):
            copy_of(r).wait()
            return c
        if isinstance(count, int):
            lax.fori_loop(0, count, wait, 0, unroll=8)
        else:
            lax.fori_loop(0, count, wait, 0)

    @pl.when(used & (f == 0))
    def _():
        @pl.when(t == 0)
        def _():
            gather(tok_ref)
        wait_rows(lambda r: row_in(r, 0), tmx)
        _rms_to(h_scr, xs_scr, g_ref)
        acc_scr[...] = jnp.zeros_like(acc_scr)

    @pl.when(used & (f == 1) & (t + 1 < nu))
    def _():
        gather(tokn_ref)

    for c in range(tmx // sub):
        @pl.when(used & (nv > c * sub))
        def _(c=c):
            sl = slice(c * sub, (c + 1) * sub)
            acc_scr[sl, :] += _swiglu_step(h_scr[sl, :], w1_ref, w3_ref, w2_ref)

    @pl.when(used & (f == nf - 1))
    def _():
        @pl.when(t > 0)
        def _():
            wait_rows(lambda r: row_out(r, 0), nv_ref[jnp.maximum(t - 1, 0)])
        y_scr[...] = acc_scr[...]

        def issue(r, c):
            row_out(r, dst_ref[0, 0, r]).start()
            return c
        lax.fori_loop(0, nv, issue, 0)

        @pl.when(t == nu - 1)
        def _():
            wait_rows(lambda r: row_out(r, 0), nv)


def _experts(x, g, tile_e, tile_nv, n_used, slot_tok, slot_dst, w1, w3, w2, *, tmx, tf):
    n, d = x.shape
    n_tiles = tile_e.shape[0]
    ff = w1.shape[2]
    nf = ff // tf
    assert nf >= 2
    sub = _pick(tmx, 512)
    kern = functools.partial(_experts_kernel, tmx=tmx, sub=sub)

    def tclamp(t, nu):
        return jnp.minimum(t, nu[0] - 1)

    def fclamp(t, f, nu):
        return jnp.where(t < nu[0], f, nf - 1)

    def wmap(t, f, te, nv, nu):
        return (te[tclamp(t, nu)], 0, fclamp(t, f, nu))

    grid_spec = pltpu.PrefetchScalarGridSpec(
        num_scalar_prefetch=3,
        grid=(n_tiles, nf),
        in_specs=[
            pl.BlockSpec((1, 1, tmx), lambda t, f, te, nv, nu: (tclamp(t, nu), 0, 0),
                         memory_space=pltpu.SMEM),
            pl.BlockSpec((1, 1, tmx), lambda t, f, te, nv, nu: (tclamp(t + 1, nu), 0, 0),
                         memory_space=pltpu.SMEM),
            pl.BlockSpec((1, 1, tmx), lambda t, f, te, nv, nu: (tclamp(t, nu), 0, 0),
                         memory_space=pltpu.SMEM),
            pl.BlockSpec((1, d), lambda t, f, te, nv, nu: (0, 0)),
            pl.BlockSpec(memory_space=pl.ANY),
            pl.BlockSpec((None, d, tf), wmap),
            pl.BlockSpec((None, d, tf), wmap),
            pl.BlockSpec((None, tf, d),
                         lambda t, f, te, nv, nu: (te[tclamp(t, nu)], fclamp(t, f, nu), 0)),
        ],
        out_specs=pl.BlockSpec(memory_space=pl.ANY),
        scratch_shapes=[
            pltpu.VMEM((tmx, d), F32),
            pltpu.VMEM((tmx, d), BF16),
            pltpu.VMEM((tmx, d), F32),
            pltpu.VMEM((tmx, d), F32),
            pltpu.SemaphoreType.DMA(()),
            pltpu.SemaphoreType.DMA(()),
        ],
    )
    vmem = (3 * tmx * d * 4 + tmx * d * 2 + 6 * d * tf * 2 + 5 * sub * tf * 4
            + 2 * sub * d * 4 + 4 * NORM_ROWS * d * 4)
    return pl.pallas_call(
        kern,
        grid_spec=grid_spec,
        out_shape=jax.ShapeDtypeStruct((TOP_K * n, d), F32),
        compiler_params=_cparams(("arbitrary", "arbitrary"), vmem),
        name="experts",
    )(tile_e, tile_nv, n_used, slot_tok, slot_tok, slot_dst, g, x, w1, w3, w2)


def _combine_kernel(x_ref, r_ref, *refs):
    y_refs, o_ref = refs[:TOP_K], refs[TOP_K]
    acc = x_ref[...]
    for k in range(TOP_K):
        acc = acc + y_refs[k][...] * r_ref[:, TOP_K + k:TOP_K + k + 1]
    o_ref[...] = acc


def _combine(x, y2, route, *, tm):
    n, d = x.shape
    nb = n // tm
    return pl.pallas_call(
        _combine_kernel,
        grid=(nb,),
        in_specs=[
            pl.BlockSpec((tm, d), lambda i: (i, 0)),
            pl.BlockSpec((tm, LANES), lambda i: (i, 0)),
        ] + [pl.BlockSpec((tm, d), functools.partial(lambda i, k: (k * nb + i, 0), k=k))
             for k in range(TOP_K)],
        out_specs=pl.BlockSpec((tm, d), lambda i: (i, 0)),
        out_shape=jax.ShapeDtypeStruct((n, d), F32),
        compiler_params=_cparams(("parallel",), 12 * tm * d * 4),
        name="combine",
    )(x, route, *([y2] * TOP_K))


def _moe_plan(route, *, n_experts, tmx, n_tiles):
    n = route.shape[0]
    flat_e = route[:, :TOP_K].astype(jnp.int32).T.reshape(-1)
    onehot = (flat_e[:, None] == jnp.arange(n_experts, dtype=jnp.int32)[None, :]).astype(jnp.int32)
    rank = jnp.sum((jnp.cumsum(onehot, axis=0) - onehot) * onehot, axis=1)
    counts = jnp.sum(onehot, axis=0)
    padded = (counts + tmx - 1) // tmx * tmx
    pad_ends = jnp.cumsum(padded)
    pad_starts = pad_ends - padded
    dest = jnp.sum(onehot * pad_starts[None, :], axis=1) + rank
    slot_a = jnp.zeros((n_tiles * tmx,), jnp.int32).at[dest].set(
        jnp.arange(n * TOP_K, dtype=jnp.int32))
    tile_start = jnp.arange(n_tiles, dtype=jnp.int32) * tmx
    tile_e = jnp.minimum(jnp.sum((tile_start[:, None] >= pad_ends[None, :]).astype(jnp.int32), axis=1),
                         n_experts - 1)
    tile_nv = jnp.clip(counts[tile_e] - (tile_start - pad_starts[tile_e]), 0, tmx).astype(jnp.int32)
    n_used = (pad_ends[-1] // tmx).astype(jnp.int32).reshape(1)
    return (tile_e, tile_nv, n_used, (slot_a % n).reshape(n_tiles, 1, tmx),
            slot_a.reshape(n_tiles, 1, tmx))


def kernel(x, norm1_g, w_in, qn_g, kn_g, b_f, w_o_fox, w_o_sb, w_gate, b_gate, w_out, norm2_g,
           ffn_w1, ffn_w3, ffn_w2, w_router, moe_w1, moe_w3, moe_w2):
    batch, seq, d = x.shape
    depth = norm1_g.shape[0]
    h_fox = qn_g.shape[1]
    wfox = h_fox * HEAD_DIM
    wsb = (w_in.shape[2] - 3 * wfox - h_fox) // 3
    h_sb = wsb // HEAD_DIM
    n_experts = w_router.shape[2]
    n = batch * seq
    scale = HEAD_DIM ** -0.5

    tn = _pick(wfox, 512)
    assert wsb % tn == 0 and d % tn == 0
    tm = _pick(n, 512)
    tm_proj = _pick(n, 1024)
    t_fox = _pick(seq, 512)
    t_sb = _pick(seq, 256)
    tf = _pick(ffn_w1.shape[2], 512)
    tfx = _pick(moe_w1.shape[3], 512)
    tmx = _pick(n, 1024)
    n_tiles = (n * TOP_K) // tmx + n_experts

    off_f = 3 * wfox
    off_sb = off_f + h_fox
    n_proj = 3 * wfox + 3 * wsb
    hg = 2 if h_fox % 2 == 0 and h_sb % 2 == 0 else 1
    hcols = hg * HEAD_DIM

    xf = x.reshape(n, d)
    for layer in range(depth):
        wl = w_in[layer]
        w_cat = jnp.concatenate([wl[:, :off_f], wl[:, off_sb:], w_gate[layer]], axis=1).astype(BF16)
        wf_pad = jnp.zeros((d, LANES), F32).at[:, :h_fox].set(wl[:, off_f:off_sb]).astype(BF16)
        bf_pad = jnp.zeros((1, LANES), F32).at[0, :h_fox].set(b_f[layer])
        cvec = jnp.concatenate([
            qn_g[layer].reshape(-1) * (scale * LOG2E), kn_g[layer].reshape(-1),
            jnp.ones((wfox,), F32),
            jnp.full((wsb,), scale, F32), jnp.ones((2 * wsb,), F32),
            jnp.ones((2 * d,), F32)]).reshape(1, -1)
        bvec = jnp.concatenate([jnp.zeros((n_proj,), F32), b_gate[layer]]).reshape(1, -1)

        y, logf = _proj(xf, norm1_g[layer].reshape(1, d), w_cat, wf_pad, cvec, bvec, bf_pad,
                        n_qk_tiles=2 * wfox // tn, n_plain_tiles=(wfox + 3 * wsb) // tn,
                        tm=tm_proj, tn=tn)
        aq, ak = _dcum(logf, batch=batch, seq=seq, n_heads=h_fox)
        o_fox = _fox_attn(y, aq, ak, batch=batch, seq=seq, n_heads=h_fox,
                          q_col=0, k_col=wfox // hcols, v_col=2 * wfox // hcols, t=t_fox, hg=hg)
        sb0 = 3 * wfox // hcols
        o_sb = _sb_attn(y, batch=batch, seq=seq, n_heads=h_sb,
                        q_col=sb0, k_col=sb0 + wsb // hcols, v_col=sb0 + 2 * wsb // hcols,
                        t=t_sb, hg=hg)
        merged = _merge(o_fox, o_sb, w_o_fox[layer].astype(BF16), w_o_sb[layer].astype(BF16), y,
                        gate_col=n_proj // tn, d=d, tm=tm_proj, tn=tn)
        xf = _out_proj(merged, w_out[layer].astype(BF16), xf, tm=tm_proj, tn=tn)

        i = layer // 2
        g2 = norm2_g[layer].reshape(1, d)
        if layer % 2 == 0:
            xf = _ffn(xf, g2, ffn_w1[i].astype(BF16), ffn_w3[i].astype(BF16),
                      ffn_w2[i].astype(BF16), tm=tm, tf=tf)
        else:
            wr = jnp.zeros((d, LANES), F32).at[:, :n_experts].set(w_router[i])
            wr_hi = wr.astype(BF16)
            wr_lo = (wr - wr_hi.astype(F32)).astype(BF16)
            route = _router(xf, g2, wr_hi, wr_lo, n_experts=n_experts, tm=tm)
            plan = _moe_plan(route, n_experts=n_experts, tmx=tmx, n_tiles=n_tiles)
            y2 = _experts(xf, g2, *plan, moe_w1[i].astype(BF16), moe_w3[i].astype(BF16),
                          moe_w2[i].astype(BF16), tmx=tmx, tf=tfx)
            xf = _combine(xf, y2, route, tm=tm)
    return xf.reshape(batch, seq, d)
```

```python
import functools
import math

import jax
import jax.numpy as jnp
from jax import lax
from jax.experimental import pallas as pl
from jax.experimental.pallas import tpu as pltpu

HEAD_DIM = 128
RMS_EPS = 1e-6
TOP_K = 2
LANES = 128
NEG_BIG = -1e30
LOG2E = math.log2(math.e)
SB_DEAD = -120.0
VMEM_LIMIT_CAP = 60000 * 1024
NORM_ROWS = 256

F32 = jnp.float32
BF16 = jnp.bfloat16
NT_DIMS = (((1,), (1,)), ((), ()))


def _cparams(semantics, vmem_bytes):
    limit = int(min(VMEM_LIMIT_CAP, max(32 * 1024 * 1024, vmem_bytes)))
    return pltpu.CompilerParams(dimension_semantics=semantics, vmem_limit_bytes=limit)


def _rms_rows(x, g):
    ms = jnp.mean(x * x, axis=-1, keepdims=True)
    return x * lax.rsqrt(ms + RMS_EPS) * g


def _rms_to(dst_ref, x_ref, g_ref):
    rows = x_ref.shape[0]
    step = min(rows, NORM_ROWS)
    for c in range(rows // step):
        sl = slice(c * step, (c + 1) * step)
        dst_ref[sl, :] = _rms_rows(x_ref[sl, :], g_ref[...]).astype(BF16)


def _log_sigmoid(z):
    return jnp.minimum(z, 0.0) - jnp.log(1.0 + jnp.exp(-jnp.abs(z)))


def _lane_tile(x, reps):
    return x if reps == 1 else jnp.concatenate([x] * reps, axis=1)


def _pick(total, pref):
    t = min(total, pref)
    while total % t:
        t //= 2
    return t


def _proj_kernel(x_ref, g_ref, w_ref, wf_ref, cvec_ref, bvec_ref, bf_ref,
                 y_ref, logf_ref, h_scr, *, n_qk_tiles, n_plain_tiles, tn):
    j = pl.program_id(1)

    @pl.when(j == 0)
    def _():
        _rms_to(h_scr, x_ref, g_ref)
        fl = jnp.dot(h_scr[...], wf_ref[...], preferred_element_type=F32)
        logf_ref[...] = _log_sigmoid(fl + bf_ref[...])

    acc = jnp.dot(h_scr[...], w_ref[...], preferred_element_type=F32)

    @pl.when(j < n_qk_tiles)
    def _():
        for c in range(tn // HEAD_DIM):
            sl = slice(c * HEAD_DIM, (c + 1) * HEAD_DIM)
            y = acc[:, sl]
            ms = jnp.mean(y * y, axis=-1, keepdims=True)
            y_ref[:, sl] = (y * lax.rsqrt(ms + RMS_EPS) * cvec_ref[:, sl]).astype(BF16)

    @pl.when((j >= n_qk_tiles) & (j < n_qk_tiles + n_plain_tiles))
    def _():
        y_ref[...] = (acc * cvec_ref[...]).astype(BF16)

    @pl.when(j >= n_qk_tiles + n_plain_tiles)
    def _():
        z = acc + bvec_ref[...]
        y_ref[...] = (1.0 / (1.0 + jnp.exp(-z))).astype(BF16)


def _wprep_kernel(win_ref, wg_ref, w_ref, wf_ref, *, off_f, n_f):
    x = win_ref[...]
    n_sb = x.shape[1] - off_f - n_f
    w_ref[:, :off_f] = x[:, :off_f].astype(BF16)
    w_ref[:, off_f:off_f + n_sb] = x[:, off_f + n_f:].astype(BF16)
    w_ref[:, off_f + n_sb:] = wg_ref[...].astype(BF16)
    lane = lax.broadcasted_iota(jnp.int32, (x.shape[0], LANES), 1)
    wf_ref[...] = jnp.where(lane < n_f, x[:, off_f:off_f + LANES], 0.0).astype(BF16)


def _wprep(w_in, w_gate, *, off_f, n_f, tr):
    depth, d, d_in = w_in.shape
    n_gate = w_gate.shape[2]
    n_out = d_in - n_f + n_gate
    kern = functools.partial(_wprep_kernel, off_f=off_f, n_f=n_f)
    vmem = 2 * tr * (d_in + n_gate) * 4 + 2 * tr * (n_out + LANES) * 2 + 3 * tr * d_in * 4
    return pl.pallas_call(
        kern,
        grid=(depth, d // tr),
        in_specs=[
            pl.BlockSpec((None, tr, d_in), lambda l, r: (l, r, 0)),
            pl.BlockSpec((None, tr, n_gate), lambda l, r: (l, r, 0)),
        ],
        out_specs=[
            pl.BlockSpec((None, tr, n_out), lambda l, r: (l, r, 0)),
            pl.BlockSpec((None, tr, LANES), lambda l, r: (l, r, 0)),
        ],
        out_shape=[
            jax.ShapeDtypeStruct((depth, d, n_out), BF16),
            jax.ShapeDtypeStruct((depth, d, LANES), BF16),
        ],
        compiler_params=_cparams(("parallel", "parallel"), vmem),
        name="wprep",
    )(w_in, w_gate)


def _proj(x, g, w, wf, cvec, bvec, bf, *, layer, n_qk_tiles, n_plain_tiles, tm, tn):
    n, d = x.shape
    n_out = w.shape[2]
    kern = functools.partial(_proj_kernel, n_qk_tiles=n_qk_tiles,
                             n_plain_tiles=n_plain_tiles, tn=tn)
    vmem = (2 * tm * d * 4 + tm * d * 2 + 2 * d * tn * 2 + 2 * d * LANES * 2
            + 2 * tm * tn * 2 + 2 * tm * LANES * 4 + 6 * tm * tn * 4 + 4 * NORM_ROWS * d * 4)
    return pl.pallas_call(
        kern,
        grid=(n // tm, n_out // tn),
        in_specs=[
            pl.BlockSpec((tm, d), lambda i, j: (i, 0)),
            pl.BlockSpec((1, d), lambda i, j: (0, 0)),
            pl.BlockSpec((None, d, tn), lambda i, j: (layer, 0, j)),
            pl.BlockSpec((None, d, LANES), lambda i, j: (layer, 0, 0)),
            pl.BlockSpec((1, tn), lambda i, j: (0, j)),
            pl.BlockSpec((1, tn), lambda i, j: (0, j)),
            pl.BlockSpec((1, LANES), lambda i, j: (0, 0)),
        ],
        out_specs=[
            pl.BlockSpec((tm, tn), lambda i, j: (i, j)),
            pl.BlockSpec((tm, LANES), lambda i, j: (i, 0)),
        ],
        out_shape=[
            jax.ShapeDtypeStruct((n, n_out), BF16),
            jax.ShapeDtypeStruct((n, LANES), F32),
        ],
        scratch_shapes=[pltpu.VMEM((tm, d), BF16)],
        compiler_params=_cparams(("parallel", "arbitrary"), vmem),
        name="proj",
    )(x, g, w, wf, cvec, bvec, bf)


def _dcum_kernel(lf_ref, aq_ref, ak_ref, *, chunk, n_heads):
    s = lf_ref.shape[0]
    r = lax.broadcasted_iota(jnp.int32, (chunk, chunk), 0)
    c = lax.broadcasted_iota(jnp.int32, (chunk, chunk), 1)
    tri = (c <= r).astype(F32)
    lane = lax.broadcasted_iota(jnp.int32, (chunk, LANES), 1)
    carry = jnp.zeros((1, LANES), F32)
    for i in range(s // chunk):
        sl = slice(i * chunk, (i + 1) * chunk)
        cs = jnp.dot(tri, lf_ref[sl, :], preferred_element_type=F32,
                     precision=lax.Precision.HIGHEST) + carry
        carry = cs[chunk - 1:chunk, :]
        d2 = cs * LOG2E
        for h in range(n_heads):
            col = jnp.broadcast_to(d2[:, h:h + 1], (chunk, LANES))
            hi = col.astype(BF16).astype(F32)
            r1 = col - hi
            mid = r1.astype(BF16).astype(F32)
            lo = (r1 - mid).astype(BF16).astype(F32)
            aq = jnp.where(lane == 0, hi, jnp.where(lane == 1, mid, jnp.where(
                lane == 2, lo, jnp.where(lane < 6, 1.0, 0.0))))
            ak = jnp.where(lane < 3, 1.0, jnp.where(lane == 3, -hi, jnp.where(
                lane == 4, -mid, jnp.where(lane == 5, -lo, 0.0))))
            hs = slice(h * HEAD_DIM, (h + 1) * HEAD_DIM)
            aq_ref[sl, hs] = aq.astype(BF16)
            ak_ref[sl, hs] = ak.astype(BF16)


def _dcum(logf, *, batch, seq, n_heads):
    chunk = _pick(seq, 256)
    kern = functools.partial(_dcum_kernel, chunk=chunk, n_heads=n_heads)
    wid = n_heads * HEAD_DIM
    return pl.pallas_call(
        kern,
        grid=(batch,),
        in_specs=[pl.BlockSpec((seq, LANES), lambda b: (b, 0))],
        out_specs=[
            pl.BlockSpec((seq, wid), lambda b: (b, 0)),
            pl.BlockSpec((seq, wid), lambda b: (b, 0)),
        ],
        out_shape=[
            jax.ShapeDtypeStruct((batch * seq, wid), BF16),
            jax.ShapeDtypeStruct((batch * seq, wid), BF16),
        ],
        compiler_params=_cparams(("parallel",), 4 * seq * LANES * 4 + 4 * seq * wid * 2),
        name="dcum",
    )(logf)


def _fox_kernel(q_ref, aq_ref, k_ref, ak_ref, v_ref, o_ref,
                kaug, vaug, s_scr, m_scr, acc_scr, *, t, hg):
    qi = pl.program_id(2)
    hd2 = 2 * HEAD_DIM

    def hcols(g):
        return slice(g * HEAD_DIM, (g + 1) * HEAD_DIM)

    @pl.when(qi == 0)
    def _():
        for g in range(hg):
            kaug[:, g * hd2:g * hd2 + HEAD_DIM] = k_ref[:, hcols(g)]
            kaug[:, g * hd2 + HEAD_DIM:(g + 1) * hd2] = ak_ref[:, hcols(g)]
            vaug[:, g * hd2:g * hd2 + HEAD_DIM] = v_ref[:, hcols(g)]
            vaug[:, g * hd2 + HEAD_DIM:(g + 1) * hd2] = jnp.ones((v_ref.shape[0], HEAD_DIM), BF16)

    qa = [jnp.concatenate([q_ref[:, hcols(g)], aq_ref[:, hcols(g)]], axis=1) for g in range(hg)]
    reps = t // LANES

    def logits(ki):
        off = pl.multiple_of(ki * t, t)
        return [lax.dot_general(qa[g], kaug[pl.ds(off, t), g * hd2:(g + 1) * hd2], NT_DIMS,
                                preferred_element_type=F32) for g in range(hg)]

    def update(ki):
        off = pl.multiple_of(ki * t, t)
        for g in range(hg):
            s = s_scr[g]
            m_old = m_scr[g]
            m_new = jnp.maximum(m_old, jnp.max(s, axis=-1, keepdims=True))
            alpha = jnp.exp2(m_old - m_new)
            p = jnp.exp2(s - _lane_tile(m_new, reps))
            pv = jnp.dot(p.astype(BF16), vaug[pl.ds(off, t), g * hd2:(g + 1) * hd2],
                         preferred_element_type=F32)
            acc_scr[g] = acc_scr[g] * _lane_tile(alpha, 2) + pv
            m_scr[g] = m_new

    m_scr[...] = jnp.full(m_scr.shape, NEG_BIG, F32)
    acc_scr[...] = jnp.zeros(acc_scr.shape, F32)
    row = lax.broadcasted_iota(jnp.int32, (t, t), 0)
    col = lax.broadcasted_iota(jnp.int32, (t, t), 1)
    for g, s in enumerate(logits(qi)):
        s_scr[g] = jnp.where(col <= row, s, NEG_BIG)

    def body(i, c):
        nxt = logits(qi - 1 - i)
        update(qi - i)
        for g in range(hg):
            s_scr[g] = nxt[g]
        return c

    lax.fori_loop(0, qi, body, 0)
    update(0)
    for g in range(hg):
        acc = acc_scr[g]
        o_ref[:, hcols(g)] = (acc[:, :HEAD_DIM] / acc[:, HEAD_DIM:]).astype(BF16)


def _fox_attn(y, aq, ak, *, batch, seq, n_heads, q_col, k_col, v_col, t, hg):
    n = y.shape[0]
    nq = seq // t
    wid = hg * HEAD_DIM
    kern = functools.partial(_fox_kernel, t=t, hg=hg)
    vmem = (10 * seq * wid * 2 + 4 * seq * wid * 2 + hg * 10 * t * t * 4
            + hg * 16 * t * 2 * HEAD_DIM * 4)
    return pl.pallas_call(
        kern,
        grid=(batch, n_heads // hg, nq),
        in_specs=[
            pl.BlockSpec((t, wid), lambda b, h, i: (b * nq + i, q_col + h)),
            pl.BlockSpec((t, wid), lambda b, h, i: (b * nq + i, h)),
            pl.BlockSpec((seq, wid), lambda b, h, i: (b, k_col + h)),
            pl.BlockSpec((seq, wid), lambda b, h, i: (b, h)),
            pl.BlockSpec((seq, wid), lambda b, h, i: (b, v_col + h)),
        ],
        out_specs=pl.BlockSpec((t, wid), lambda b, h, i: (b * nq + i, h)),
        out_shape=jax.ShapeDtypeStruct((n, n_heads * HEAD_DIM), BF16),
        scratch_shapes=[
            pltpu.VMEM((seq, 2 * wid), BF16),
            pltpu.VMEM((seq, 2 * wid), BF16),
            pltpu.VMEM((hg, t, t), F32),
            pltpu.VMEM((hg, t, LANES), F32),
            pltpu.VMEM((hg, t, 2 * HEAD_DIM), F32),
        ],
        compiler_params=_cparams(("parallel", "parallel", "arbitrary"), vmem),
        name="fox_attn",
    )(y, aq, y, ak, y)


def _sb_kernel(q_ref, k_ref, v_ref, o_ref, carry_scr, acc_scr, *, t, hg):
    qi = pl.program_id(2)
    row = lax.broadcasted_iota(jnp.int32, (t, t), 0)
    col = lax.broadcasted_iota(jnp.int32, (t, t), 1)
    upper = (row > col).astype(BF16)
    reps = t // LANES

    def hcols(g):
        return slice(g * HEAD_DIM, (g + 1) * HEAD_DIM)

    carry_scr[...] = jnp.zeros(carry_scr.shape, F32)
    acc_scr[...] = jnp.zeros(acc_scr.shape, F32)

    def step(ki, masked):
        off = pl.multiple_of(ki * t, t)
        for g in range(hg):
            k = k_ref[pl.ds(off, t), hcols(g)]
            v = v_ref[pl.ds(off, t), hcols(g)]
            z = lax.dot_general(q_ref[:, hcols(g)], k, NT_DIMS, preferred_element_type=F32)
            log_beta = _log_sigmoid(z)
            log_1m = log_beta - z
            if masked:
                keep = col < row
                log_1m = jnp.where(keep, log_1m, 0.0)
            hi = log_1m.astype(BF16)
            lo = (log_1m - hi.astype(F32)).astype(BF16)
            tail = (jnp.dot(hi, upper, preferred_element_type=F32)
                    + jnp.dot(lo, upper, preferred_element_type=F32))
            carry = carry_scr[g]
            a = jnp.exp(log_beta + tail + _lane_tile(carry, reps))
            if masked:
                a = jnp.where(keep, a, 0.0)
            acc_scr[g] += jnp.dot(a.astype(BF16), v, preferred_element_type=F32)
            carry_scr[g] = carry + jnp.sum(log_1m, axis=-1, keepdims=True)

    step(qi, True)

    def alive(c):
        i, worst = c
        return (i < qi) & (worst > SB_DEAD)

    def body(c):
        i, _ = c
        step(qi - 1 - i, False)
        return i + 1, jnp.max(carry_scr[...])

    lax.while_loop(alive, body, (jnp.int32(0), jnp.max(carry_scr[...])))
    for g in range(hg):
        o_ref[:, hcols(g)] = acc_scr[g].astype(BF16)


def _sb_attn(y, *, batch, seq, n_heads, q_col, k_col, v_col, t, hg):
    n = y.shape[0]
    nq = seq // t
    wid = hg * HEAD_DIM
    kern = functools.partial(_sb_kernel, t=t, hg=hg)
    vmem = 4 * seq * wid * 2 + hg * 14 * t * t * 4 + hg * 16 * t * LANES * 4
    return pl.pallas_call(
        kern,
        grid=(batch, n_heads // hg, nq),
        in_specs=[
            pl.BlockSpec((t, wid), lambda b, h, i: (b * nq + i, q_col + h)),
            pl.BlockSpec((seq, wid), lambda b, h, i: (b, k_col + h)),
            pl.BlockSpec((seq, wid), lambda b, h, i: (b, v_col + h)),
        ],
        out_specs=pl.BlockSpec((t, wid), lambda b, h, i: (b * nq + i, h)),
        out_shape=jax.ShapeDtypeStruct((n, n_heads * HEAD_DIM), BF16),
        scratch_shapes=[pltpu.VMEM((hg, t, LANES), F32), pltpu.VMEM((hg, t, HEAD_DIM), F32)],
        compiler_params=_cparams(("parallel", "parallel", "arbitrary"), vmem),
        name="sb_attn",
    )(y, y, y)


def _merge_kernel(of_ref, os_ref, wf_ref, ws_ref, gf_ref, gs_ref, o_ref):
    a = jnp.dot(of_ref[...], wf_ref[...], preferred_element_type=F32)
    b = jnp.dot(os_ref[...], ws_ref[...], preferred_element_type=F32)
    o_ref[...] = (gf_ref[...].astype(F32) * a + gs_ref[...].astype(F32) * b).astype(BF16)


def _merge(o_fox, o_sb, w_of, w_os, y, *, gate_col, d, tm, tn):
    n, wf = o_fox.shape
    ws = o_sb.shape[1]
    nd = d // tn
    vmem = 2 * tm * (wf + ws) * 2 + 2 * (wf + ws) * tn * 2 + 6 * tm * tn * 2 + 4 * tm * tn * 4
    return pl.pallas_call(
        _merge_kernel,
        grid=(n // tm, nd),
        in_specs=[
            pl.BlockSpec((tm, wf), lambda i, j: (i, 0)),
            pl.BlockSpec((tm, ws), lambda i, j: (i, 0)),
            pl.BlockSpec((wf, tn), lambda i, j: (0, j)),
            pl.BlockSpec((ws, tn), lambda i, j: (0, j)),
            pl.BlockSpec((tm, tn), lambda i, j: (i, gate_col + j)),
            pl.BlockSpec((tm, tn), lambda i, j: (i, gate_col + nd + j)),
        ],
        out_specs=pl.BlockSpec((tm, tn), lambda i, j: (i, j)),
        out_shape=jax.ShapeDtypeStruct((n, d), BF16),
        compiler_params=_cparams(("parallel", "arbitrary"), vmem),
        name="merge",
    )(o_fox, o_sb, w_of, w_os, y, y)


def _out_proj_kernel(m_ref, w_ref, x_ref, o_ref):
    o_ref[...] = x_ref[...] + jnp.dot(m_ref[...], w_ref[...], preferred_element_type=F32)


def _out_proj(merged, w_out, x, *, tm, tn):
    n, d = merged.shape
    vmem = 2 * tm * d * 2 + 2 * d * tn * 2 + 6 * tm * tn * 4
    return pl.pallas_call(
        _out_proj_kernel,
        grid=(n // tm, d // tn),
        in_specs=[
            pl.BlockSpec((tm, d), lambda i, j: (i, 0)),
            pl.BlockSpec((d, tn), lambda i, j: (0, j)),
            pl.BlockSpec((tm, tn), lambda i, j: (i, j)),
        ],
        out_specs=pl.BlockSpec((tm, tn), lambda i, j: (i, j)),
        out_shape=jax.ShapeDtypeStruct((n, d), F32),
        compiler_params=_cparams(("parallel", "arbitrary"), vmem),
        name="out_proj",
    )(merged, w_out, x)


def _swiglu(h, w1, w3, w2):
    a = jnp.dot(h, w1, preferred_element_type=F32)
    b = jnp.dot(h, w3, preferred_element_type=F32)
    mid = (a * (1.0 / (1.0 + jnp.exp(-a))) * b).astype(BF16)
    return jnp.dot(mid, w2, preferred_element_type=F32)


def _swiglu_step(h, w1_ref, w3_ref, w2_ref):
    return _swiglu(h, w1_ref[...], w3_ref[...], w2_ref[...])


def _ffn_kernel(x_ref, g_ref, w1_ref, w3_ref, w2_ref, o_ref, h_scr):
    f = pl.program_id(1)

    @pl.when(f == 0)
    def _():
        _rms_to(h_scr, x_ref, g_ref)
        o_ref[...] = x_ref[...]

    o_ref[...] += _swiglu_step(h_scr[...], w1_ref, w3_ref, w2_ref)


def _ffn(x, g, w1, w3, w2, *, tm, tf):
    n, d = x.shape
    ff = w1.shape[1]
    vmem = 4 * tm * d * 4 + tm * d * 2 + 6 * d * tf * 2 + 5 * tm * tf * 4 + 2 * tm * d * 4
    return pl.pallas_call(
        _ffn_kernel,
        grid=(n // tm, ff // tf),
        in_specs=[
            pl.BlockSpec((tm, d), lambda i, f: (i, 0)),
            pl.BlockSpec((1, d), lambda i, f: (0, 0)),
            pl.BlockSpec((d, tf), lambda i, f: (0, f)),
            pl.BlockSpec((d, tf), lambda i, f: (0, f)),
            pl.BlockSpec((tf, d), lambda i, f: (f, 0)),
        ],
        out_specs=pl.BlockSpec((tm, d), lambda i, f: (i, 0)),
        out_shape=jax.ShapeDtypeStruct((n, d), F32),
        scratch_shapes=[pltpu.VMEM((tm, d), BF16)],
        compiler_params=_cparams(("parallel", "arbitrary"), vmem),
        name="ffn",
    )(x, g, w1, w3, w2)


def _router_kernel(x_ref, g_ref, whi_ref, wlo_ref, r_ref, *, n_experts):
    h = _rms_rows(x_ref[...], g_ref[...])
    h_hi = h.astype(BF16)
    h_lo = (h - h_hi.astype(F32)).astype(BF16)
    logits = (jnp.dot(h_hi, whi_ref[...], preferred_element_type=F32)
              + jnp.dot(h_hi, wlo_ref[...], preferred_element_type=F32)
              + jnp.dot(h_lo, whi_ref[...], preferred_element_type=F32))
    lane = lax.broadcasted_iota(jnp.int32, logits.shape, 1)
    logits = jnp.where(lane < n_experts, logits, NEG_BIG)
    m1 = jnp.max(logits, axis=-1, keepdims=True)
    i1 = jnp.min(jnp.where(logits == m1, lane, LANES), axis=-1, keepdims=True)
    rest = jnp.where(lane == i1, NEG_BIG, logits)
    m2 = jnp.max(rest, axis=-1, keepdims=True)
    i2 = jnp.min(jnp.where(rest == m2, lane, LANES), axis=-1, keepdims=True)
    e = jnp.exp(m2 - m1)
    w1 = 1.0 / (1.0 + e)
    w2 = e * w1
    out = jnp.where(lane == 0, i1.astype(F32),
          jnp.where(lane == 1, i2.astype(F32),
          jnp.where(lane == 2, w1, jnp.where(lane == 3, w2, 0.0))))
    r_ref[...] = out


def _router(x, g, w_hi, w_lo, *, n_experts, tm):
    n, d = x.shape
    kern = functools.partial(_router_kernel, n_experts=n_experts)
    vmem = 2 * tm * d * 4 + 4 * d * LANES * 2 + 4 * tm * d * 4 + 8 * tm * LANES * 4
    return pl.pallas_call(
        kern,
        grid=(n // tm,),
        in_specs=[
            pl.BlockSpec((tm, d), lambda i: (i, 0)),
            pl.BlockSpec((1, d), lambda i: (0, 0)),
            pl.BlockSpec((d, LANES), lambda i: (0, 0)),
            pl.BlockSpec((d, LANES), lambda i: (0, 0)),
        ],
        out_specs=pl.BlockSpec((tm, LANES), lambda i: (i, 0)),
        out_shape=jax.ShapeDtypeStruct((n, LANES), F32),
        compiler_params=_cparams(("parallel",), vmem),
        name="router",
    )(x, g, w_hi, w_lo)


def _experts_kernel(te_ref, nv_ref, nu_ref,
                    tok_ref, tokn_ref, dst_ref, g_ref, x_hbm, w1_ref, w3_ref, w2_ref,
                    out_hbm, xs_scr, h_scr, acc_scr, y_scr, gsem, ssem, *, tmx, sub):
    t = pl.program_id(0)
    f = pl.program_id(1)
    nf = pl.num_programs(1)
    nu = nu_ref[0]
    used = t < nu
    nv = nv_ref[t]

    def row_in(r, tok):
        return pltpu.make_async_copy(x_hbm.at[pl.ds(tok, 1), :], xs_scr.at[pl.ds(r, 1), :], gsem)

    def row_out(r, dst):
        return pltpu.make_async_copy(y_scr.at[pl.ds(r, 1), :], out_hbm.at[pl.ds(dst, 1), :], ssem)

    def gather(toks_ref):
        def issue(r, c):
            row_in(r, toks_ref[0, 0, r]).start()
            return c
        lax.fori_loop(0, tmx, issue, 0, unroll=8)

    def wait_rows(copy_of, count):
        def wait(r, c):
            copy_of(r).wait()
            return c
        if isinstance(count, int):
            lax.fori_loop(0, count, wait, 0, unroll=8)
        else:
            lax.fori_loop(0, count, wait, 0)

    @pl.when(used & (f == 0))
    def _():
        @pl.when(t == 0)
        def _():
            gather(tok_ref)
        wait_rows(lambda r: row_in(r, 0), tmx)
        _rms_to(h_scr, xs_scr, g_ref)
        acc_scr[...] = jnp.zeros_like(acc_scr)

    @pl.when(used & (f == 1) & (t + 1 < nu))
    def _():
        gather(tokn_ref)

    for c in range(tmx // sub):
        @pl.when(used & (nv > c * sub))
        def _(c=c):
            sl = slice(c * sub, (c + 1) * sub)
            acc_scr[sl, :] += _swiglu(h_scr[sl, :], w1_ref[...].astype(BF16),
                                      w3_ref[...].astype(BF16), w2_ref[...].astype(BF16))

    @pl.when(used & (f == nf - 1))
    def _():
        @pl.when(t > 0)
        def _():
            wait_rows(lambda r: row_out(r, 0), nv_ref[jnp.maximum(t - 1, 0)])
        y_scr[...] = acc_scr[...]

        def issue(r, c):
            row_out(r, dst_ref[0, 0, r]).start()
            return c
        lax.fori_loop(0, nv, issue, 0)

        @pl.when(t == nu - 1)
        def _():
            wait_rows(lambda r: row_out(r, 0), nv)


def _experts(x, g, tile_e, tile_nv, n_used, slot_tok, slot_dst, w1, w3, w2, *, layer, tmx, tf):
    n, d = x.shape
    n_tiles = tile_e.shape[0]
    ff = w1.shape[3]
    nf = ff // tf
    assert nf >= 2
    sub = _pick(tmx, 512)
    wbytes = w1.dtype.itemsize
    kern = functools.partial(_experts_kernel, tmx=tmx, sub=sub)

    def tclamp(t, nu):
        return jnp.minimum(t, nu[0] - 1)

    def fclamp(t, f, nu):
        return jnp.where(t < nu[0], f, nf - 1)

    def wmap(t, f, te, nv, nu):
        return (layer, te[tclamp(t, nu)], 0, fclamp(t, f, nu))

    grid_spec = pltpu.PrefetchScalarGridSpec(
        num_scalar_prefetch=3,
        grid=(n_tiles, nf),
        in_specs=[
            pl.BlockSpec((1, 1, tmx), lambda t, f, te, nv, nu: (tclamp(t, nu), 0, 0),
                         memory_space=pltpu.SMEM),
            pl.BlockSpec((1, 1, tmx), lambda t, f, te, nv, nu: (tclamp(t + 1, nu), 0, 0),
                         memory_space=pltpu.SMEM),
            pl.BlockSpec((1, 1, tmx), lambda t, f, te, nv, nu: (tclamp(t, nu), 0, 0),
                         memory_space=pltpu.SMEM),
            pl.BlockSpec((1, d), lambda t, f, te, nv, nu: (0, 0)),
            pl.BlockSpec(memory_space=pl.ANY),
            pl.BlockSpec((None, None, d, tf), wmap),
            pl.BlockSpec((None, None, d, tf), wmap),
            pl.BlockSpec((None, None, tf, d),
                         lambda t, f, te, nv, nu: (layer, te[tclamp(t, nu)], fclamp(t, f, nu), 0)),
        ],
        out_specs=pl.BlockSpec(memory_space=pl.ANY),
        scratch_shapes=[
            pltpu.VMEM((tmx, d), F32),
            pltpu.VMEM((tmx, d), BF16),
            pltpu.VMEM((tmx, d), F32),
            pltpu.VMEM((tmx, d), F32),
            pltpu.SemaphoreType.DMA(()),
            pltpu.SemaphoreType.DMA(()),
        ],
    )
    vmem = (3 * tmx * d * 4 + tmx * d * 2 + 6 * d * tf * wbytes + 3 * d * tf * 2
            + 5 * sub * tf * 4 + 2 * sub * d * 4 + 4 * NORM_ROWS * d * 4)
    return pl.pallas_call(
        kern,
        grid_spec=grid_spec,
        out_shape=jax.ShapeDtypeStruct((TOP_K * n, d), F32),
        compiler_params=_cparams(("arbitrary", "arbitrary"), vmem),
        name="experts",
    )(tile_e, tile_nv, n_used, slot_tok, slot_tok, slot_dst, g, x, w1, w3, w2)


def _combine_kernel(x_ref, r_ref, *refs):
    y_refs, o_ref = refs[:TOP_K], refs[TOP_K]
    acc = x_ref[...]
    for k in range(TOP_K):
        acc = acc + y_refs[k][...] * r_ref[:, TOP_K + k:TOP_K + k + 1]
    o_ref[...] = acc


def _combine(x, y2, route, *, tm):
    n, d = x.shape
    nb = n // tm
    return pl.pallas_call(
        _combine_kernel,
        grid=(nb,),
        in_specs=[
            pl.BlockSpec((tm, d), lambda i: (i, 0)),
            pl.BlockSpec((tm, LANES), lambda i: (i, 0)),
        ] + [pl.BlockSpec((tm, d), functools.partial(lambda i, k: (k * nb + i, 0), k=k))
             for k in range(TOP_K)],
        out_specs=pl.BlockSpec((tm, d), lambda i: (i, 0)),
        out_shape=jax.ShapeDtypeStruct((n, d), F32),
        compiler_params=_cparams(("parallel",), 12 * tm * d * 4),
        name="combine",
    )(x, route, *([y2] * TOP_K))


def _moe_plan(route, *, n_experts, tmx, n_tiles):
    n = route.shape[0]
    flat_e = route[:, :TOP_K].astype(jnp.int32).T.reshape(-1)
    onehot = (flat_e[:, None] == jnp.arange(n_experts, dtype=jnp.int32)[None, :]).astype(jnp.int32)
    rank = jnp.sum((jnp.cumsum(onehot, axis=0) - onehot) * onehot, axis=1)
    counts = jnp.sum(onehot, axis=0)
    padded = (counts + tmx - 1) // tmx * tmx
    pad_ends = jnp.cumsum(padded)
    pad_starts = pad_ends - padded
    dest = jnp.sum(onehot * pad_starts[None, :], axis=1) + rank
    slot_a = jnp.zeros((n_tiles * tmx,), jnp.int32).at[dest].set(
        jnp.arange(n * TOP_K, dtype=jnp.int32))
    tile_start = jnp.arange(n_tiles, dtype=jnp.int32) * tmx
    tile_e = jnp.minimum(jnp.sum((tile_start[:, None] >= pad_ends[None, :]).astype(jnp.int32), axis=1),
                         n_experts - 1)
    tile_nv = jnp.clip(counts[tile_e] - (tile_start - pad_starts[tile_e]), 0, tmx).astype(jnp.int32)
    n_used = (pad_ends[-1] // tmx).astype(jnp.int32).reshape(1)
    return (tile_e, tile_nv, n_used, (slot_a % n).reshape(n_tiles, 1, tmx),
            slot_a.reshape(n_tiles, 1, tmx))


def kernel(x, norm1_g, w_in, qn_g, kn_g, b_f, w_o_fox, w_o_sb, w_gate, b_gate, w_out, norm2_g,
           ffn_w1, ffn_w3, ffn_w2, w_router, moe_w1, moe_w3, moe_w2):
    batch, seq, d = x.shape
    depth = norm1_g.shape[0]
    h_fox = qn_g.shape[1]
    wfox = h_fox * HEAD_DIM
    wsb = (w_in.shape[2] - 3 * wfox - h_fox) // 3
    h_sb = wsb // HEAD_DIM
    n_experts = w_router.shape[2]
    n = batch * seq
    scale = HEAD_DIM ** -0.5

    tn = _pick(wfox, 512)
    assert wsb % tn == 0 and d % tn == 0
    tm = _pick(n, 512)
    tm_proj = _pick(n, 1024)
    t_fox = _pick(seq, 512)
    t_sb = _pick(seq, 256)
    tf = _pick(ffn_w1.shape[2], 512)
    tfx = _pick(moe_w1.shape[3], 256)
    tmx = _pick(n, 1024)
    n_tiles = (n * TOP_K) // tmx + n_experts

    off_f = 3 * wfox
    n_proj = 3 * wfox + 3 * wsb
    hg = 2 if h_fox % 2 == 0 and h_sb % 2 == 0 else 1
    hg_sb = 4 if hg == 2 and h_sb % 4 == 0 and wfox % (4 * HEAD_DIM) == 0 else hg
    hcols = hg * HEAD_DIM
    hcols_sb = hg_sb * HEAD_DIM

    xf = x.reshape(n, d)
    w_cat, wf_pad = _wprep(w_in, w_gate, off_f=off_f, n_f=h_fox, tr=_pick(d, 128))
    for layer in range(depth):
        bf_pad = jnp.zeros((1, LANES), F32).at[0, :h_fox].set(b_f[layer])
        cvec = jnp.concatenate([
            qn_g[layer].reshape(-1) * (scale * LOG2E), kn_g[layer].reshape(-1),
            jnp.ones((wfox,), F32),
            jnp.full((wsb,), scale, F32), jnp.ones((2 * wsb,), F32),
            jnp.ones((2 * d,), F32)]).reshape(1, -1)
        bvec = jnp.concatenate([jnp.zeros((n_proj,), F32), b_gate[layer]]).reshape(1, -1)

        y, logf = _proj(xf, norm1_g[layer].reshape(1, d), w_cat, wf_pad, cvec, bvec, bf_pad,
                        layer=layer, n_qk_tiles=2 * wfox // tn,
                        n_plain_tiles=(wfox + 3 * wsb) // tn, tm=tm_proj, tn=tn)
        aq, ak = _dcum(logf, batch=batch, seq=seq, n_heads=h_fox)
        o_fox = _fox_attn(y, aq, ak, batch=batch, seq=seq, n_heads=h_fox,
                          q_col=0, k_col=wfox // hcols, v_col=2 * wfox // hcols, t=t_fox, hg=hg)
        sb0 = 3 * wfox // hcols_sb
        o_sb = _sb_attn(y, batch=batch, seq=seq, n_heads=h_sb, q_col=sb0,
                        k_col=sb0 + wsb // hcols_sb, v_col=sb0 + 2 * wsb // hcols_sb,
                        t=t_sb, hg=hg_sb)
        merged = _merge(o_fox, o_sb, w_o_fox[layer].astype(BF16), w_o_sb[layer].astype(BF16), y,
                        gate_col=n_proj // tn, d=d, tm=tm_proj, tn=tn)
        xf = _out_proj(merged, w_out[layer].astype(BF16), xf, tm=tm_proj, tn=tn)

        i = layer // 2
        g2 = norm2_g[layer].reshape(1, d)
        if layer % 2 == 0:
            xf = _ffn(xf, g2, ffn_w1[i].astype(BF16), ffn_w3[i].astype(BF16),
                      ffn_w2[i].astype(BF16), tm=tm, tf=tf)
        else:
            wr = jnp.zeros((d, LANES), F32).at[:, :n_experts].set(w_router[i])
            wr_hi = wr.astype(BF16)
            wr_lo = (wr - wr_hi.astype(F32)).astype(BF16)
            route = _router(xf, g2, wr_hi, wr_lo, n_experts=n_experts, tm=tm)
            plan = _moe_plan(route, n_experts=n_experts, tmx=tmx, n_tiles=n_tiles)
            y2 = _experts(xf, g2, *plan, moe_w1, moe_w3, moe_w2, layer=i, tmx=tmx, tf=tfx)
            xf = _combine(xf, y2, route, tm=tm)
    return xf.reshape(batch, seq, d)
```

```python
import functools
import math

import jax
import jax.numpy as jnp
from jax import lax
from jax.experimental import pallas as pl
from jax.experimental.pallas import tpu as pltpu

HEAD_DIM = 128
RMS_EPS = 1e-6
TOP_K = 2
LANES = 128
NEG_BIG = -1e30
LOG2E = math.log2(math.e)
SB_DEAD = -120.0
VMEM_LIMIT_CAP = 60000 * 1024
NORM_ROWS = 256

F32 = jnp.float32
BF16 = jnp.bfloat16
NT_DIMS = (((1,), (1,)), ((), ()))


def _cparams(semantics, vmem_bytes):
    limit = int(min(VMEM_LIMIT_CAP, max(32 * 1024 * 1024, vmem_bytes)))
    return pltpu.CompilerParams(dimension_semantics=semantics, vmem_limit_bytes=limit)


def _rms_rows(x, g):
    ms = jnp.mean(x * x, axis=-1, keepdims=True)
    return x * lax.rsqrt(ms + RMS_EPS) * g


def _rms_to(dst_ref, x_ref, g_ref):
    rows = x_ref.shape[0]
    step = min(rows, NORM_ROWS)
    for c in range(rows // step):
        sl = slice(c * step, (c + 1) * step)
        dst_ref[sl, :] = _rms_rows(x_ref[sl, :], g_ref[...]).astype(BF16)


def _log_sigmoid(z):
    return jnp.minimum(z, 0.0) - jnp.log(1.0 + jnp.exp(-jnp.abs(z)))


def _lane_tile(x, reps):
    return x if reps == 1 else jnp.concatenate([x] * reps, axis=1)


def _pick(total, pref):
    t = min(total, pref)
    while total % t:
        t //= 2
    return t


def _proj_kernel(x_ref, g_ref, w_ref, wf_ref, cvec_ref, bvec_ref, bf_ref,
                 y_ref, logf_ref, h_scr, *, n_qk_tiles, n_plain_tiles, tn):
    j = pl.program_id(1)

    @pl.when(j == 0)
    def _():
        _rms_to(h_scr, x_ref, g_ref)
        fl = jnp.dot(h_scr[...], wf_ref[...], preferred_element_type=F32)
        logf_ref[...] = _log_sigmoid(fl + bf_ref[...])

    acc = jnp.dot(h_scr[...], w_ref[...], preferred_element_type=F32)

    @pl.when(j < n_qk_tiles)
    def _():
        for c in range(tn // HEAD_DIM):
            sl = slice(c * HEAD_DIM, (c + 1) * HEAD_DIM)
            y = acc[:, sl]
            ms = jnp.mean(y * y, axis=-1, keepdims=True)
            y_ref[:, sl] = (y * lax.rsqrt(ms + RMS_EPS) * cvec_ref[:, sl]).astype(BF16)

    @pl.when((j >= n_qk_tiles) & (j < n_qk_tiles + n_plain_tiles))
    def _():
        y_ref[...] = (acc * cvec_ref[...]).astype(BF16)

    @pl.when(j >= n_qk_tiles + n_plain_tiles)
    def _():
        z = acc + bvec_ref[...]
        y_ref[...] = (1.0 / (1.0 + jnp.exp(-z))).astype(BF16)


def _wprep_kernel(win_ref, wg_ref, w_ref, wf_ref, *, off_f, n_f):
    x = win_ref[...]
    n_sb = x.shape[1] - off_f - n_f
    w_ref[:, :off_f] = x[:, :off_f].astype(BF16)
    w_ref[:, off_f:off_f + n_sb] = x[:, off_f + n_f:].astype(BF16)
    w_ref[:, off_f + n_sb:] = wg_ref[...].astype(BF16)
    lane = lax.broadcasted_iota(jnp.int32, (x.shape[0], LANES), 1)
    wf_ref[...] = jnp.where(lane < n_f, x[:, off_f:off_f + LANES], 0.0).astype(BF16)


def _wprep(w_in, w_gate, *, off_f, n_f, tr):
    depth, d, d_in = w_in.shape
    n_gate = w_gate.shape[2]
    n_out = d_in - n_f + n_gate
    kern = functools.partial(_wprep_kernel, off_f=off_f, n_f=n_f)
    vmem = 2 * tr * (d_in + n_gate) * 4 + 2 * tr * (n_out + LANES) * 2 + 3 * tr * d_in * 4
    return pl.pallas_call(
        kern,
        grid=(depth, d // tr),
        in_specs=[
            pl.BlockSpec((None, tr, d_in), lambda l, r: (l, r, 0)),
            pl.BlockSpec((None, tr, n_gate), lambda l, r: (l, r, 0)),
        ],
        out_specs=[
            pl.BlockSpec((None, tr, n_out), lambda l, r: (l, r, 0)),
            pl.BlockSpec((None, tr, LANES), lambda l, r: (l, r, 0)),
        ],
        out_shape=[
            jax.ShapeDtypeStruct((depth, d, n_out), BF16),
            jax.ShapeDtypeStruct((depth, d, LANES), BF16),
        ],
        compiler_params=_cparams(("parallel", "parallel"), vmem),
        name="wprep",
    )(w_in, w_gate)


def _proj(x, g, w, wf, cvec, bvec, bf, *, layer, n_qk_tiles, n_plain_tiles, tm, tn):
    n, d = x.shape
    n_out = w.shape[2]
    kern = functools.partial(_proj_kernel, n_qk_tiles=n_qk_tiles,
                             n_plain_tiles=n_plain_tiles, tn=tn)
    vmem = (2 * tm * d * 4 + tm * d * 2 + 2 * d * tn * 2 + 2 * d * LANES * 2
            + 2 * tm * tn * 2 + 2 * tm * LANES * 4 + 6 * tm * tn * 4 + 4 * NORM_ROWS * d * 4)
    return pl.pallas_call(
        kern,
        grid=(n // tm, n_out // tn),
        in_specs=[
            pl.BlockSpec((tm, d), lambda i, j: (i, 0)),
            pl.BlockSpec((1, d), lambda i, j: (0, 0)),
            pl.BlockSpec((None, d, tn), lambda i, j: (layer, 0, j)),
            pl.BlockSpec((None, d, LANES), lambda i, j: (layer, 0, 0)),
            pl.BlockSpec((1, tn), lambda i, j: (0, j)),
            pl.BlockSpec((1, tn), lambda i, j: (0, j)),
            pl.BlockSpec((1, LANES), lambda i, j: (0, 0)),
        ],
        out_specs=[
            pl.BlockSpec((tm, tn), lambda i, j: (i, j)),
            pl.BlockSpec((tm, LANES), lambda i, j: (i, 0)),
        ],
        out_shape=[
            jax.ShapeDtypeStruct((n, n_out), BF16),
            jax.ShapeDtypeStruct((n, LANES), F32),
        ],
        scratch_shapes=[pltpu.VMEM((tm, d), BF16)],
        compiler_params=_cparams(("parallel", "arbitrary"), vmem),
        name="proj",
    )(x, g, w, wf, cvec, bvec, bf)


def _dcum_kernel(lf_ref, aq_ref, ak_ref, *, chunk, n_heads):
    s = lf_ref.shape[0]
    r = lax.broadcasted_iota(jnp.int32, (chunk, chunk), 0)
    c = lax.broadcasted_iota(jnp.int32, (chunk, chunk), 1)
    tri = (c <= r).astype(F32)
    lane = lax.broadcasted_iota(jnp.int32, (chunk, LANES), 1)
    carry = jnp.zeros((1, LANES), F32)
    for i in range(s // chunk):
        sl = slice(i * chunk, (i + 1) * chunk)
        cs = jnp.dot(tri, lf_ref[sl, :], preferred_element_type=F32,
                     precision=lax.Precision.HIGHEST) + carry
        carry = cs[chunk - 1:chunk, :]
        d2 = cs * LOG2E
        for h in range(n_heads):
            col = jnp.broadcast_to(d2[:, h:h + 1], (chunk, LANES))
            hi = col.astype(BF16).astype(F32)
            r1 = col - hi
            mid = r1.astype(BF16).astype(F32)
            lo = (r1 - mid).astype(BF16).astype(F32)
            aq = jnp.where(lane == 0, hi, jnp.where(lane == 1, mid, jnp.where(
                lane == 2, lo, jnp.where(lane < 6, 1.0, 0.0))))
            ak = jnp.where(lane < 3, 1.0, jnp.where(lane == 3, -hi, jnp.where(
                lane == 4, -mid, jnp.where(lane == 5, -lo, 0.0))))
            hs = slice(h * HEAD_DIM, (h + 1) * HEAD_DIM)
            aq_ref[sl, hs] = aq.astype(BF16)
            ak_ref[sl, hs] = ak.astype(BF16)


def _dcum(logf, *, batch, seq, n_heads):
    chunk = _pick(seq, 256)
    kern = functools.partial(_dcum_kernel, chunk=chunk, n_heads=n_heads)
    wid = n_heads * HEAD_DIM
    return pl.pallas_call(
        kern,
        grid=(batch,),
        in_specs=[pl.BlockSpec((seq, LANES), lambda b: (b, 0))],
        out_specs=[
            pl.BlockSpec((seq, wid), lambda b: (b, 0)),
            pl.BlockSpec((seq, wid), lambda b: (b, 0)),
        ],
        out_shape=[
            jax.ShapeDtypeStruct((batch * seq, wid), BF16),
            jax.ShapeDtypeStruct((batch * seq, wid), BF16),
        ],
        compiler_params=_cparams(("parallel",), 4 * seq * LANES * 4 + 4 * seq * wid * 2),
        name="dcum",
    )(logf)


def _fox_kernel(q_ref, aq_ref, k_ref, ak_ref, v_ref, o_ref,
                kaug, vaug, s_scr, m_scr, acc_scr, *, t, hg):
    qi = pl.program_id(2)
    hd2 = 2 * HEAD_DIM

    def hcols(g):
        return slice(g * HEAD_DIM, (g + 1) * HEAD_DIM)

    @pl.when(qi == 0)
    def _():
        for g in range(hg):
            kaug[:, g * hd2:g * hd2 + HEAD_DIM] = k_ref[:, hcols(g)]
            kaug[:, g * hd2 + HEAD_DIM:(g + 1) * hd2] = ak_ref[:, hcols(g)]
            vaug[:, g * hd2:g * hd2 + HEAD_DIM] = v_ref[:, hcols(g)]
            vaug[:, g * hd2 + HEAD_DIM:(g + 1) * hd2] = jnp.ones((v_ref.shape[0], HEAD_DIM), BF16)

    qa = [jnp.concatenate([q_ref[:, hcols(g)], aq_ref[:, hcols(g)]], axis=1) for g in range(hg)]
    reps = t // LANES

    def logits(ki):
        off = pl.multiple_of(ki * t, t)
        return [lax.dot_general(qa[g], kaug[pl.ds(off, t), g * hd2:(g + 1) * hd2], NT_DIMS,
                                preferred_element_type=F32) for g in range(hg)]

    def update(ki):
        off = pl.multiple_of(ki * t, t)
        for g in range(hg):
            s = s_scr[g]
            m_old = m_scr[g]
            m_new = jnp.maximum(m_old, jnp.max(s, axis=-1, keepdims=True))
            alpha = jnp.exp2(m_old - m_new)
            p = jnp.exp2(s - _lane_tile(m_new, reps))
            pv = jnp.dot(p.astype(BF16), vaug[pl.ds(off, t), g * hd2:(g + 1) * hd2],
                         preferred_element_type=F32)
            acc_scr[g] = acc_scr[g] * _lane_tile(alpha, 2) + pv
            m_scr[g] = m_new

    m_scr[...] = jnp.full(m_scr.shape, NEG_BIG, F32)
    acc_scr[...] = jnp.zeros(acc_scr.shape, F32)
    row = lax.broadcasted_iota(jnp.int32, (t, t), 0)
    col = lax.broadcasted_iota(jnp.int32, (t, t), 1)
    for g, s in enumerate(logits(qi)):
        s_scr[g] = jnp.where(col <= row, s, NEG_BIG)

    def body(i, c):
        nxt = logits(qi - 1 - i)
        update(qi - i)
        for g in range(hg):
            s_scr[g] = nxt[g]
        return c

    lax.fori_loop(0, qi, body, 0)
    update(0)
    for g in range(hg):
        acc = acc_scr[g]
        o_ref[:, hcols(g)] = (acc[:, :HEAD_DIM] / acc[:, HEAD_DIM:]).astype(BF16)


def _fox_attn(y, aq, ak, *, batch, seq, n_heads, q_col, k_col, v_col, t, hg):
    n = y.shape[0]
    nq = seq // t
    wid = hg * HEAD_DIM
    kern = functools.partial(_fox_kernel, t=t, hg=hg)
    vmem = (10 * seq * wid * 2 + 4 * seq * wid * 2 + hg * 10 * t * t * 4
            + hg * 16 * t * 2 * HEAD_DIM * 4)
    return pl.pallas_call(
        kern,
        grid=(batch, n_heads // hg, nq),
        in_specs=[
            pl.BlockSpec((t, wid), lambda b, h, i: (b * nq + i, q_col + h)),
            pl.BlockSpec((t, wid), lambda b, h, i: (b * nq + i, h)),
            pl.BlockSpec((seq, wid), lambda b, h, i: (b, k_col + h)),
            pl.BlockSpec((seq, wid), lambda b, h, i: (b, h)),
            pl.BlockSpec((seq, wid), lambda b, h, i: (b, v_col + h)),
        ],
        out_specs=pl.BlockSpec((t, wid), lambda b, h, i: (b * nq + i, h)),
        out_shape=jax.ShapeDtypeStruct((n, n_heads * HEAD_DIM), BF16),
        scratch_shapes=[
            pltpu.VMEM((seq, 2 * wid), BF16),
            pltpu.VMEM((seq, 2 * wid), BF16),
            pltpu.VMEM((hg, t, t), F32),
            pltpu.VMEM((hg, t, LANES), F32),
            pltpu.VMEM((hg, t, 2 * HEAD_DIM), F32),
        ],
        compiler_params=_cparams(("parallel", "parallel", "arbitrary"), vmem),
        name="fox_attn",
    )(y, aq, y, ak, y)


def _sb_kernel(q_ref, k_ref, v_ref, o_ref, carry_scr, acc_scr, *, t, hg):
    qi = pl.program_id(2)
    row = lax.broadcasted_iota(jnp.int32, (t, t), 0)
    col = lax.broadcasted_iota(jnp.int32, (t, t), 1)
    upper = (row > col).astype(BF16)
    reps = t // LANES

    def hcols(g):
        return slice(g * HEAD_DIM, (g + 1) * HEAD_DIM)

    carry_scr[...] = jnp.zeros(carry_scr.shape, F32)
    acc_scr[...] = jnp.zeros(acc_scr.shape, F32)

    def step(ki, masked):
        off = pl.multiple_of(ki * t, t)
        for g in range(hg):
            k = k_ref[pl.ds(off, t), hcols(g)]
            v = v_ref[pl.ds(off, t), hcols(g)]
            z = lax.dot_general(q_ref[:, hcols(g)], k, NT_DIMS, preferred_element_type=F32)
            log_beta = _log_sigmoid(z)
            log_1m = log_beta - z
            if masked:
                keep = col < row
                log_1m = jnp.where(keep, log_1m, 0.0)
            hi = log_1m.astype(BF16)
            lo = (log_1m - hi.astype(F32)).astype(BF16)
            tail = (jnp.dot(hi, upper, preferred_element_type=F32)
                    + jnp.dot(lo, upper, preferred_element_type=F32))
            carry = carry_scr[g]
            a = jnp.exp(log_beta + tail + _lane_tile(carry, reps))
            if masked:
                a = jnp.where(keep, a, 0.0)
            acc_scr[g] += jnp.dot(a.astype(BF16), v, preferred_element_type=F32)
            carry_scr[g] = carry + jnp.sum(log_1m, axis=-1, keepdims=True)

    step(qi, True)

    def alive(c):
        i, worst = c
        return (i < qi) & (worst > SB_DEAD)

    def body(c):
        i, _ = c
        step(qi - 1 - i, False)
        return i + 1, jnp.max(carry_scr[...])

    lax.while_loop(alive, body, (jnp.int32(0), jnp.max(carry_scr[...])))
    for g in range(hg):
        o_ref[:, hcols(g)] = acc_scr[g].astype(BF16)


def _sb_attn(y, *, batch, seq, n_heads, q_col, k_col, v_col, t, hg):
    n = y.shape[0]
    nq = seq // t
    wid = hg * HEAD_DIM
    kern = functools.partial(_sb_kernel, t=t, hg=hg)
    vmem = 4 * seq * wid * 2 + hg * 14 * t * t * 4 + hg * 16 * t * LANES * 4
    return pl.pallas_call(
        kern,
        grid=(batch, n_heads // hg, nq),
        in_specs=[
            pl.BlockSpec((t, wid), lambda b, h, i: (b * nq + i, q_col + h)),
            pl.BlockSpec((seq, wid), lambda b, h, i: (b, k_col + h)),
            pl.BlockSpec((seq, wid), lambda b, h, i: (b, v_col + h)),
        ],
        out_specs=pl.BlockSpec((t, wid), lambda b, h, i: (b * nq + i, h)),
        out_shape=jax.ShapeDtypeStruct((n, n_heads * HEAD_DIM), BF16),
        scratch_shapes=[pltpu.VMEM((hg, t, LANES), F32), pltpu.VMEM((hg, t, HEAD_DIM), F32)],
        compiler_params=_cparams(("parallel", "parallel", "arbitrary"), vmem),
        name="sb_attn",
    )(y, y, y)


def _merge_kernel(of_ref, os_ref, wf_ref, ws_ref, gf_ref, gs_ref, o_ref):
    a = jnp.dot(of_ref[...], wf_ref[...], preferred_element_type=F32)
    b = jnp.dot(os_ref[...], ws_ref[...], preferred_element_type=F32)
    o_ref[...] = (gf_ref[...].astype(F32) * a + gs_ref[...].astype(F32) * b).astype(BF16)


def _merge(o_fox, o_sb, w_of, w_os, y, *, gate_col, d, tm, tn):
    n, wf = o_fox.shape
    ws = o_sb.shape[1]
    nd = d // tn
    vmem = 2 * tm * (wf + ws) * 2 + 2 * (wf + ws) * tn * 2 + 6 * tm * tn * 2 + 4 * tm * tn * 4
    return pl.pallas_call(
        _merge_kernel,
        grid=(n // tm, nd),
        in_specs=[
            pl.BlockSpec((tm, wf), lambda i, j: (i, 0)),
            pl.BlockSpec((tm, ws), lambda i, j: (i, 0)),
            pl.BlockSpec((wf, tn), lambda i, j: (0, j)),
            pl.BlockSpec((ws, tn), lambda i, j: (0, j)),
            pl.BlockSpec((tm, tn), lambda i, j: (i, gate_col + j)),
            pl.BlockSpec((tm, tn), lambda i, j: (i, gate_col + nd + j)),
        ],
        out_specs=pl.BlockSpec((tm, tn), lambda i, j: (i, j)),
        out_shape=jax.ShapeDtypeStruct((n, d), BF16),
        compiler_params=_cparams(("parallel", "arbitrary"), vmem),
        name="merge",
    )(o_fox, o_sb, w_of, w_os, y, y)


def _out_proj_kernel(m_ref, w_ref, x_ref, o_ref):
    o_ref[...] = x_ref[...] + jnp.dot(m_ref[...], w_ref[...], preferred_element_type=F32)


def _out_proj(merged, w_out, x, *, tm, tn):
    n, d = merged.shape
    vmem = 2 * tm * d * 2 + 2 * d * tn * 2 + 6 * tm * tn * 4
    return pl.pallas_call(
        _out_proj_kernel,
        grid=(n // tm, d // tn),
        in_specs=[
            pl.BlockSpec((tm, d), lambda i, j: (i, 0)),
            pl.BlockSpec((d, tn), lambda i, j: (0, j)),
            pl.BlockSpec((tm, tn), lambda i, j: (i, j)),
        ],
        out_specs=pl.BlockSpec((tm, tn), lambda i, j: (i, j)),
        out_shape=jax.ShapeDtypeStruct((n, d), F32),
        compiler_params=_cparams(("parallel", "arbitrary"), vmem),
        name="out_proj",
    )(merged, w_out, x)


def _swiglu(h, w1, w3, w2):
    a = jnp.dot(h, w1, preferred_element_type=F32)
    b = jnp.dot(h, w3, preferred_element_type=F32)
    mid = (a * (1.0 / (1.0 + jnp.exp(-a))) * b).astype(BF16)
    return jnp.dot(mid, w2, preferred_element_type=F32)


def _swiglu_step(h, w1_ref, w3_ref, w2_ref):
    return _swiglu(h, w1_ref[...], w3_ref[...], w2_ref[...])


def _ffn_kernel(x_ref, g_ref, w1_ref, w3_ref, w2_ref, o_ref, h_scr):
    f = pl.program_id(1)

    @pl.when(f == 0)
    def _():
        _rms_to(h_scr, x_ref, g_ref)
        o_ref[...] = x_ref[...]

    o_ref[...] += _swiglu_step(h_scr[...], w1_ref, w3_ref, w2_ref)


def _ffn(x, g, w1, w3, w2, *, tm, tf):
    n, d = x.shape
    ff = w1.shape[1]
    vmem = 4 * tm * d * 4 + tm * d * 2 + 6 * d * tf * 2 + 5 * tm * tf * 4 + 2 * tm * d * 4
    return pl.pallas_call(
        _ffn_kernel,
        grid=(n // tm, ff // tf),
        in_specs=[
            pl.BlockSpec((tm, d), lambda i, f: (i, 0)),
            pl.BlockSpec((1, d), lambda i, f: (0, 0)),
            pl.BlockSpec((d, tf), lambda i, f: (0, f)),
            pl.BlockSpec((d, tf), lambda i, f: (0, f)),
            pl.BlockSpec((tf, d), lambda i, f: (f, 0)),
        ],
        out_specs=pl.BlockSpec((tm, d), lambda i, f: (i, 0)),
        out_shape=jax.ShapeDtypeStruct((n, d), F32),
        scratch_shapes=[pltpu.VMEM((tm, d), BF16)],
        compiler_params=_cparams(("parallel", "arbitrary"), vmem),
        name="ffn",
    )(x, g, w1, w3, w2)


def _router_kernel(x_ref, g_ref, whi_ref, wlo_ref, r_ref, *, n_experts):
    h = _rms_rows(x_ref[...], g_ref[...])
    h_hi = h.astype(BF16)
    h_lo = (h - h_hi.astype(F32)).astype(BF16)
    logits = (jnp.dot(h_hi, whi_ref[...], preferred_element_type=F32)
              + jnp.dot(h_hi, wlo_ref[...], preferred_element_type=F32)
              + jnp.dot(h_lo, whi_ref[...], preferred_element_type=F32))
    lane = lax.broadcasted_iota(jnp.int32, logits.shape, 1)
    logits = jnp.where(lane < n_experts, logits, NEG_BIG)
    m1 = jnp.max(logits, axis=-1, keepdims=True)
    i1 = jnp.min(jnp.where(logits == m1, lane, LANES), axis=-1, keepdims=True)
    rest = jnp.where(lane == i1, NEG_BIG, logits)
    m2 = jnp.max(rest, axis=-1, keepdims=True)
    i2 = jnp.min(jnp.where(rest == m2, lane, LANES), axis=-1, keepdims=True)
    e = jnp.exp(m2 - m1)
    w1 = 1.0 / (1.0 + e)
    w2 = e * w1
    out = jnp.where(lane == 0, i1.astype(F32),
          jnp.where(lane == 1, i2.astype(F32),
          jnp.where(lane == 2, w1, jnp.where(lane == 3, w2, 0.0))))
    r_ref[...] = out


def _router(x, g, w_hi, w_lo, *, n_experts, tm):
    n, d = x.shape
    kern = functools.partial(_router_kernel, n_experts=n_experts)
    vmem = 2 * tm * d * 4 + 4 * d * LANES * 2 + 4 * tm * d * 4 + 8 * tm * LANES * 4
    return pl.pallas_call(
        kern,
        grid=(n // tm,),
        in_specs=[
            pl.BlockSpec((tm, d), lambda i: (i, 0)),
            pl.BlockSpec((1, d), lambda i: (0, 0)),
            pl.BlockSpec((d, LANES), lambda i: (0, 0)),
            pl.BlockSpec((d, LANES), lambda i: (0, 0)),
        ],
        out_specs=pl.BlockSpec((tm, LANES), lambda i: (i, 0)),
        out_shape=jax.ShapeDtypeStruct((n, LANES), F32),
        compiler_params=_cparams(("parallel",), vmem),
        name="router",
    )(x, g, w_hi, w_lo)


def _experts_kernel(te_ref, nv_ref, nu_ref,
                    tok_ref, tokn_ref, dst_ref, g_ref, x_hbm, w1_ref, w3_ref, w2_ref,
                    out_hbm, xs_scr, h_scr, acc_scr, y_scr, gsem, ssem, *, tmx, sub):
    t = pl.program_id(0)
    f = pl.program_id(1)
    nf = pl.num_programs(1)
    nu = nu_ref[0]
    used = t < nu
    nv = nv_ref[t]

    def row_in(r, tok):
        return pltpu.make_async_copy(x_hbm.at[pl.ds(tok, 1), :], xs_scr.at[pl.ds(r, 1), :], gsem)

    def row_out(r, dst):
        return pltpu.make_async_copy(y_scr.at[pl.ds(r, 1), :], out_hbm.at[pl.ds(dst, 1), :], ssem)

    def gather(toks_ref):
        def issue(r, c):
            row_in(r, toks_ref[0, 0, r]).start()
            return c
        lax.fori_loop(0, tmx, issue, 0, unroll=8)

    def wait_rows(copy_of, count):
        def wait(r, c):
            copy_of(r).wait()
            return c
        if isinstance(count, int):
            lax.fori_loop(0, count, wait, 0, unroll=8)
        else:
            lax.fori_loop(0, count, wait, 0)

    @pl.when(used & (f == 0))
    def _():
        @pl.when(t == 0)
        def _():
            gather(tok_ref)
        wait_rows(lambda r: row_in(r, 0), tmx)
        _rms_to(h_scr, xs_scr, g_ref)
        acc_scr[...] = jnp.zeros_like(acc_scr)

    @pl.when(used & (f == 1) & (t + 1 < nu))
    def _():
        gather(tokn_ref)

    for c in range(tmx // sub):
        @pl.when(used & (nv > c * sub))
        def _(c=c):
            sl = slice(c * sub, (c + 1) * sub)
            acc_scr[sl, :] += _swiglu(h_scr[sl, :], w1_ref[...].astype(BF16),
                                      w3_ref[...].astype(BF16), w2_ref[...].astype(BF16))

    @pl.when(used & (f == nf - 1))
    def _():
        @pl.when(t > 0)
        def _():
            wait_rows(lambda r: row_out(r, 0), nv_ref[jnp.maximum(t - 1, 0)])
        y_scr[...] = acc_scr[...]

        def issue(r, c):
            row_out(r, dst_ref[0, 0, r]).start()
            return c
        lax.fori_loop(0, nv, issue, 0)

        @pl.when(t == nu - 1)
        def _():
            wait_rows(lambda r: row_out(r, 0), nv)


def _experts(x, g, tile_e, tile_nv, n_used, slot_tok, slot_dst, w1, w3, w2, *, layer, tmx, tf):
    n, d = x.shape
    n_tiles = tile_e.shape[0]
    ff = w1.shape[3]
    nf = ff // tf
    assert nf >= 2
    sub = _pick(tmx, 512)
    wbytes = w1.dtype.itemsize
    kern = functools.partial(_experts_kernel, tmx=tmx, sub=sub)

    def tclamp(t, nu):
        return jnp.minimum(t, nu[0] - 1)

    def fclamp(t, f, nu):
        return jnp.where(t < nu[0], f, nf - 1)

    def wmap(t, f, te, nv, nu):
        return (layer, te[tclamp(t, nu)], 0, fclamp(t, f, nu))

    grid_spec = pltpu.PrefetchScalarGridSpec(
        num_scalar_prefetch=3,
        grid=(n_tiles, nf),
        in_specs=[
            pl.BlockSpec((1, 1, tmx), lambda t, f, te, nv, nu: (tclamp(t, nu), 0, 0),
                         memory_space=pltpu.SMEM),
            pl.BlockSpec((1, 1, tmx), lambda t, f, te, nv, nu: (tclamp(t + 1, nu), 0, 0),
                         memory_space=pltpu.SMEM),
            pl.BlockSpec((1, 1, tmx), lambda t, f, te, nv, nu: (tclamp(t, nu), 0, 0),
                         memory_space=pltpu.SMEM),
            pl.BlockSpec((1, d), lambda t, f, te, nv, nu: (0, 0)),
            pl.BlockSpec(memory_space=pl.ANY),
            pl.BlockSpec((None, None, d, tf), wmap),
            pl.BlockSpec((None, None, d, tf), wmap),
            pl.BlockSpec((None, None, tf, d),
                         lambda t, f, te, nv, nu: (layer, te[tclamp(t, nu)], fclamp(t, f, nu), 0)),
        ],
        out_specs=pl.BlockSpec(memory_space=pl.ANY),
        scratch_shapes=[
            pltpu.VMEM((tmx, d), F32),
            pltpu.VMEM((tmx, d), BF16),
            pltpu.VMEM((tmx, d), F32),
            pltpu.VMEM((tmx, d), F32),
            pltpu.SemaphoreType.DMA(()),
            pltpu.SemaphoreType.DMA(()),
        ],
    )
    vmem = (3 * tmx * d * 4 + tmx * d * 2 + 6 * d * tf * wbytes + 3 * d * tf * 2
            + 5 * sub * tf * 4 + 2 * sub * d * 4 + 4 * NORM_ROWS * d * 4)
    return pl.pallas_call(
        kern,
        grid_spec=grid_spec,
        out_shape=jax.ShapeDtypeStruct((TOP_K * n, d), F32),
        compiler_params=_cparams(("arbitrary", "arbitrary"), vmem),
        name="experts",
    )(tile_e, tile_nv, n_used, slot_tok, slot_tok, slot_dst, g, x, w1, w3, w2)


def _combine_kernel(x_ref, r_ref, *refs):
    y_refs, o_ref = refs[:TOP_K], refs[TOP_K]
    acc = x_ref[...]
    for k in range(TOP_K):
        acc = acc + y_refs[k][...] * r_ref[:, TOP_K + k:TOP_K + k + 1]
    o_ref[...] = acc


def _combine(x, y2, route, *, tm):
    n, d = x.shape
    nb = n // tm
    return pl.pallas_call(
        _combine_kernel,
        grid=(nb,),
        in_specs=[
            pl.BlockSpec((tm, d), lambda i: (i, 0)),
            pl.BlockSpec((tm, LANES), lambda i: (i, 0)),
        ] + [pl.BlockSpec((tm, d), functools.partial(lambda i, k: (k * nb + i, 0), k=k))
             for k in range(TOP_K)],
        out_specs=pl.BlockSpec((tm, d), lambda i: (i, 0)),
        out_shape=jax.ShapeDtypeStruct((n, d), F32),
        compiler_params=_cparams(("parallel",), 12 * tm * d * 4),
        name="combine",
    )(x, route, *([y2] * TOP_K))


def _moe_plan(route, *, n_experts, tmx, n_tiles):
    n = route.shape[0]
    flat_e = route[:, :TOP_K].astype(jnp.int32).T.reshape(-1)
    onehot = (flat_e[:, None] == jnp.arange(n_experts, dtype=jnp.int32)[None, :]).astype(jnp.int32)
    rank = jnp.sum((jnp.cumsum(onehot, axis=0) - onehot) * onehot, axis=1)
    counts = jnp.sum(onehot, axis=0)
    padded = (counts + tmx - 1) // tmx * tmx
    pad_ends = jnp.cumsum(padded)
    pad_starts = pad_ends - padded
    dest = jnp.sum(onehot * pad_starts[None, :], axis=1) + rank
    slot_a = jnp.zeros((n_tiles * tmx,), jnp.int32).at[dest].set(
        jnp.arange(n * TOP_K, dtype=jnp.int32))
    tile_start = jnp.arange(n_tiles, dtype=jnp.int32) * tmx
    tile_e = jnp.minimum(jnp.sum((tile_start[:, None] >= pad_ends[None, :]).astype(jnp.int32), axis=1),
                         n_experts - 1)
    tile_nv = jnp.clip(counts[tile_e] - (tile_start - pad_starts[tile_e]), 0, tmx).astype(jnp.int32)
    n_used = (pad_ends[-1] // tmx).astype(jnp.int32).reshape(1)
    return (tile_e, tile_nv, n_used, (slot_a % n).reshape(n_tiles, 1, tmx),
            slot_a.reshape(n_tiles, 1, tmx))


def kernel(x, norm1_g, w_in, qn_g, kn_g, b_f, w_o_fox, w_o_sb, w_gate, b_gate, w_out, norm2_g,
           ffn_w1, ffn_w3, ffn_w2, w_router, moe_w1, moe_w3, moe_w2):
    batch, seq, d = x.shape
    depth = norm1_g.shape[0]
    h_fox = qn_g.shape[1]
    wfox = h_fox * HEAD_DIM
    wsb = (w_in.shape[2] - 3 * wfox - h_fox) // 3
    h_sb = wsb // HEAD_DIM
    n_experts = w_router.shape[2]
    n = batch * seq
    scale = HEAD_DIM ** -0.5

    tn = _pick(wfox, 1024)
    assert wsb % tn == 0 and d % tn == 0
    tm = _pick(n, 512)
    tm_proj = _pick(n, 1024)
    t_fox = _pick(seq, 512)
    t_sb = _pick(seq, 256)
    tf = _pick(ffn_w1.shape[2], 512)
    tfx = _pick(moe_w1.shape[3], 256)
    tmx = _pick(n, 1024)
    n_tiles = (n * TOP_K) // tmx + n_experts

    off_f = 3 * wfox
    n_proj = 3 * wfox + 3 * wsb
    hg = 2 if h_fox % 2 == 0 and h_sb % 2 == 0 else 1
    hg_sb = 4 if hg == 2 and h_sb % 4 == 0 and wfox % (4 * HEAD_DIM) == 0 else hg
    hcols = hg * HEAD_DIM
    hcols_sb = hg_sb * HEAD_DIM

    xf = x.reshape(n, d)
    w_cat, wf_pad = _wprep(w_in, w_gate, off_f=off_f, n_f=h_fox, tr=_pick(d, 128))
    for layer in range(depth):
        bf_pad = jnp.zeros((1, LANES), F32).at[0, :h_fox].set(b_f[layer])
        cvec = jnp.concatenate([
            qn_g[layer].reshape(-1) * (scale * LOG2E), kn_g[layer].reshape(-1),
            jnp.ones((wfox,), F32),
            jnp.full((wsb,), scale, F32), jnp.ones((2 * wsb,), F32),
            jnp.ones((2 * d,), F32)]).reshape(1, -1)
        bvec = jnp.concatenate([jnp.zeros((n_proj,), F32), b_gate[layer]]).reshape(1, -1)

        y, logf = _proj(xf, norm1_g[layer].reshape(1, d), w_cat, wf_pad, cvec, bvec, bf_pad,
                        layer=layer, n_qk_tiles=2 * wfox // tn,
                        n_plain_tiles=(wfox + 3 * wsb) // tn, tm=tm_proj, tn=tn)
        aq, ak = _dcum(logf, batch=batch, seq=seq, n_heads=h_fox)
        o_fox = _fox_attn(y, aq, ak, batch=batch, seq=seq, n_heads=h_fox,
                          q_col=0, k_col=wfox // hcols, v_col=2 * wfox // hcols, t=t_fox, hg=hg)
        sb0 = 3 * wfox // hcols_sb
        o_sb = _sb_attn(y, batch=batch, seq=seq, n_heads=h_sb, q_col=sb0,
                        k_col=sb0 + wsb // hcols_sb, v_col=sb0 + 2 * wsb // hcols_sb,
                        t=t_sb, hg=hg_sb)
        merged = _merge(o_fox, o_sb, w_o_fox[layer].astype(BF16), w_o_sb[layer].astype(BF16), y,
                        gate_col=n_proj // tn, d=d, tm=tm_proj, tn=tn)
        xf = _out_proj(merged, w_out[layer].astype(BF16), xf, tm=tm_proj, tn=tn)

        i = layer // 2
        g2 = norm2_g[layer].reshape(1, d)
        if layer % 2 == 0:
            xf = _ffn(xf, g2, ffn_w1[i].astype(BF16), ffn_w3[i].astype(BF16),
                      ffn_w2[i].astype(BF16), tm=tm, tf=tf)
        else:
            wr = jnp.zeros((d, LANES), F32).at[:, :n_experts].set(w_router[i])
            wr_hi = wr.astype(BF16)
            wr_lo = (wr - wr_hi.astype(F32)).astype(BF16)
            route = _router(xf, g2, wr_hi, wr_lo, n_experts=n_experts, tm=tm)
            plan = _moe_plan(route, n_experts=n_experts, tmx=tmx, n_tiles=n_tiles)
            y2 = _experts(xf, g2, *plan, moe_w1, moe_w3, moe_w2, layer=i, tmx=tmx, tf=tfx)
            xf = _combine(xf, y2, route, tm=tm)
    return xf.reshape(batch, seq, d)
```

```python
import functools
import math

import jax
import jax.numpy as jnp
from jax import lax
from jax.experimental import pallas as pl
from jax.experimental.pallas import tpu as pltpu

HEAD_DIM = 128
RMS_EPS = 1e-6
TOP_K = 2
LANES = 128
NEG_BIG = -1e30
LOG2E = math.log2(math.e)
SB_DEAD = -120.0
VMEM_LIMIT_CAP = 60000 * 1024
NORM_ROWS = 256

F32 = jnp.float32
BF16 = jnp.bfloat16
NT_DIMS = (((1,), (1,)), ((), ()))


def _cparams(semantics, vmem_bytes):
    limit = int(min(VMEM_LIMIT_CAP, max(32 * 1024 * 1024, vmem_bytes)))
    return pltpu.CompilerParams(dimension_semantics=semantics, vmem_limit_bytes=limit)


def _rms_rows(x, g):
    ms = jnp.mean(x * x, axis=-1, keepdims=True)
    return x * lax.rsqrt(ms + RMS_EPS) * g


def _rms_to(dst_ref, x_ref, g_ref):
    rows = x_ref.shape[0]
    step = min(rows, NORM_ROWS)
    for c in range(rows // step):
        sl = slice(c * step, (c + 1) * step)
        dst_ref[sl, :] = _rms_rows(x_ref[sl, :], g_ref[...]).astype(BF16)


def _log_sigmoid(z):
    return jnp.minimum(z, 0.0) - jnp.log(1.0 + jnp.exp(-jnp.abs(z)))


def _lane_tile(x, reps):
    return x if reps == 1 else jnp.concatenate([x] * reps, axis=1)


def _pick(total, pref):
    t = min(total, pref)
    while total % t:
        t //= 2
    return t


def _proj_kernel(x_ref, g_ref, w_ref, wf_ref, cvec_ref, bvec_ref, bf_ref,
                 y_ref, logf_ref, h_scr, *, n_qk_tiles, n_plain_tiles, tn):
    j = pl.program_id(1)

    @pl.when(j == 0)
    def _():
        _rms_to(h_scr, x_ref, g_ref)
        fl = jnp.dot(h_scr[...], wf_ref[...], preferred_element_type=F32)
        logf_ref[...] = _log_sigmoid(fl + bf_ref[...])

    acc = jnp.dot(h_scr[...], w_ref[...], preferred_element_type=F32)

    @pl.when(j < n_qk_tiles)
    def _():
        for c in range(tn // HEAD_DIM):
            sl = slice(c * HEAD_DIM, (c + 1) * HEAD_DIM)
            y = acc[:, sl]
            ms = jnp.mean(y * y, axis=-1, keepdims=True)
            y_ref[:, sl] = (y * lax.rsqrt(ms + RMS_EPS) * cvec_ref[:, sl]).astype(BF16)

    @pl.when((j >= n_qk_tiles) & (j < n_qk_tiles + n_plain_tiles))
    def _():
        y_ref[...] = (acc * cvec_ref[...]).astype(BF16)

    @pl.when(j >= n_qk_tiles + n_plain_tiles)
    def _():
        z = acc + bvec_ref[...]
        y_ref[...] = (1.0 / (1.0 + jnp.exp(-z))).astype(BF16)


def _wprep_kernel(win_ref, wg_ref, w_ref, wf_ref, *, off_f, n_f):
    x = win_ref[...]
    n_sb = x.shape[1] - off_f - n_f
    w_ref[:, :off_f] = x[:, :off_f].astype(BF16)
    w_ref[:, off_f:off_f + n_sb] = x[:, off_f + n_f:].astype(BF16)
    w_ref[:, off_f + n_sb:] = wg_ref[...].astype(BF16)
    lane = lax.broadcasted_iota(jnp.int32, (x.shape[0], LANES), 1)
    wf_ref[...] = jnp.where(lane < n_f, x[:, off_f:off_f + LANES], 0.0).astype(BF16)


def _wprep(w_in, w_gate, *, off_f, n_f, tr):
    depth, d, d_in = w_in.shape
    n_gate = w_gate.shape[2]
    n_out = d_in - n_f + n_gate
    kern = functools.partial(_wprep_kernel, off_f=off_f, n_f=n_f)
    vmem = 2 * tr * (d_in + n_gate) * 4 + 2 * tr * (n_out + LANES) * 2 + 3 * tr * d_in * 4
    return pl.pallas_call(
        kern,
        grid=(depth, d // tr),
        in_specs=[
            pl.BlockSpec((None, tr, d_in), lambda l, r: (l, r, 0)),
            pl.BlockSpec((None, tr, n_gate), lambda l, r: (l, r, 0)),
        ],
        out_specs=[
            pl.BlockSpec((None, tr, n_out), lambda l, r: (l, r, 0)),
            pl.BlockSpec((None, tr, LANES), lambda l, r: (l, r, 0)),
        ],
        out_shape=[
            jax.ShapeDtypeStruct((depth, d, n_out), BF16),
            jax.ShapeDtypeStruct((depth, d, LANES), BF16),
        ],
        compiler_params=_cparams(("parallel", "parallel"), vmem),
        name="wprep",
    )(w_in, w_gate)


def _proj(x, g, w, wf, cvec, bvec, bf, *, layer, n_qk_tiles, n_plain_tiles, tm, tn):
    n, d = x.shape
    n_out = w.shape[2]
    kern = functools.partial(_proj_kernel, n_qk_tiles=n_qk_tiles,
                             n_plain_tiles=n_plain_tiles, tn=tn)
    vmem = (2 * tm * d * 4 + tm * d * 2 + 2 * d * tn * 2 + 2 * d * LANES * 2
            + 2 * tm * tn * 2 + 2 * tm * LANES * 4 + 6 * tm * tn * 4 + 4 * NORM_ROWS * d * 4)
    return pl.pallas_call(
        kern,
        grid=(n // tm, n_out // tn),
        in_specs=[
            pl.BlockSpec((tm, d), lambda i, j: (i, 0)),
            pl.BlockSpec((1, d), lambda i, j: (0, 0)),
            pl.BlockSpec((None, d, tn), lambda i, j: (layer, 0, j)),
            pl.BlockSpec((None, d, LANES), lambda i, j: (layer, 0, 0)),
            pl.BlockSpec((1, tn), lambda i, j: (0, j)),
            pl.BlockSpec((1, tn), lambda i, j: (0, j)),
            pl.BlockSpec((1, LANES), lambda i, j: (0, 0)),
        ],
        out_specs=[
            pl.BlockSpec((tm, tn), lambda i, j: (i, j)),
            pl.BlockSpec((tm, LANES), lambda i, j: (i, 0)),
        ],
        out_shape=[
            jax.ShapeDtypeStruct((n, n_out), BF16),
            jax.ShapeDtypeStruct((n, LANES), F32),
        ],
        scratch_shapes=[pltpu.VMEM((tm, d), BF16)],
        compiler_params=_cparams(("parallel", "arbitrary"), vmem),
        name="proj",
    )(x, g, w, wf, cvec, bvec, bf)


def _dcum_kernel(lf_ref, aq_ref, ak_ref, *, chunk, n_heads):
    s = lf_ref.shape[0]
    r = lax.broadcasted_iota(jnp.int32, (chunk, chunk), 0)
    c = lax.broadcasted_iota(jnp.int32, (chunk, chunk), 1)
    tri = (c <= r).astype(F32)
    lane = lax.broadcasted_iota(jnp.int32, (chunk, LANES), 1)
    carry = jnp.zeros((1, LANES), F32)
    for i in range(s // chunk):
        sl = slice(i * chunk, (i + 1) * chunk)
        cs = jnp.dot(tri, lf_ref[sl, :], preferred_element_type=F32,
                     precision=lax.Precision.HIGHEST) + carry
        carry = cs[chunk - 1:chunk, :]
        d2 = cs * LOG2E
        for h in range(n_heads):
            col = jnp.broadcast_to(d2[:, h:h + 1], (chunk, LANES))
            hi = col.astype(BF16).astype(F32)
            r1 = col - hi
            mid = r1.astype(BF16).astype(F32)
            lo = (r1 - mid).astype(BF16).astype(F32)
            aq = jnp.where(lane == 0, hi, jnp.where(lane == 1, mid, jnp.where(
                lane == 2, lo, jnp.where(lane < 6, 1.0, 0.0))))
            ak = jnp.where(lane < 3, 1.0, jnp.where(lane == 3, -hi, jnp.where(
                lane == 4, -mid, jnp.where(lane == 5, -lo, 0.0))))
            hs = slice(h * HEAD_DIM, (h + 1) * HEAD_DIM)
            aq_ref[sl, hs] = aq.astype(BF16)
            ak_ref[sl, hs] = ak.astype(BF16)


def _dcum(logf, *, batch, seq, n_heads):
    chunk = _pick(seq, 256)
    kern = functools.partial(_dcum_kernel, chunk=chunk, n_heads=n_heads)
    wid = n_heads * HEAD_DIM
    return pl.pallas_call(
        kern,
        grid=(batch,),
        in_specs=[pl.BlockSpec((seq, LANES), lambda b: (b, 0))],
        out_specs=[
            pl.BlockSpec((seq, wid), lambda b: (b, 0)),
            pl.BlockSpec((seq, wid), lambda b: (b, 0)),
        ],
        out_shape=[
            jax.ShapeDtypeStruct((batch * seq, wid), BF16),
            jax.ShapeDtypeStruct((batch * seq, wid), BF16),
        ],
        compiler_params=_cparams(("parallel",), 4 * seq * LANES * 4 + 4 * seq * wid * 2),
        name="dcum",
    )(logf)


def _fox_kernel(q_ref, aq_ref, k_ref, ak_ref, v_ref, o_ref,
                kaug, vaug, s_scr, m_scr, acc_scr, *, t, hg):
    qi = pl.program_id(2)
    hd2 = 2 * HEAD_DIM

    def hcols(g):
        return slice(g * HEAD_DIM, (g + 1) * HEAD_DIM)

    @pl.when(qi == 0)
    def _():
        for g in range(hg):
            kaug[:, g * hd2:g * hd2 + HEAD_DIM] = k_ref[:, hcols(g)]
            kaug[:, g * hd2 + HEAD_DIM:(g + 1) * hd2] = ak_ref[:, hcols(g)]
            vaug[:, g * hd2:g * hd2 + HEAD_DIM] = v_ref[:, hcols(g)]
            vaug[:, g * hd2 + HEAD_DIM:(g + 1) * hd2] = jnp.ones((v_ref.shape[0], HEAD_DIM), BF16)

    qa = [jnp.concatenate([q_ref[:, hcols(g)], aq_ref[:, hcols(g)]], axis=1) for g in range(hg)]
    reps = t // LANES

    def logits(ki):
        off = pl.multiple_of(ki * t, t)
        return [lax.dot_general(qa[g], kaug[pl.ds(off, t), g * hd2:(g + 1) * hd2], NT_DIMS,
                                preferred_element_type=F32) for g in range(hg)]

    def update(ki):
        off = pl.multiple_of(ki * t, t)
        for g in range(hg):
            s = s_scr[g]
            m_old = m_scr[g]
            m_new = jnp.maximum(m_old, jnp.max(s, axis=-1, keepdims=True))
            alpha = jnp.exp2(m_old - m_new)
            p = jnp.exp2(s - _lane_tile(m_new, reps))
            pv = jnp.dot(p.astype(BF16), vaug[pl.ds(off, t), g * hd2:(g + 1) * hd2],
                         preferred_element_type=F32)
            acc_scr[g] = acc_scr[g] * _lane_tile(alpha, 2) + pv
            m_scr[g] = m_new

    m_scr[...] = jnp.full(m_scr.shape, NEG_BIG, F32)
    acc_scr[...] = jnp.zeros(acc_scr.shape, F32)
    row = lax.broadcasted_iota(jnp.int32, (t, t), 0)
    col = lax.broadcasted_iota(jnp.int32, (t, t), 1)
    for g, s in enumerate(logits(qi)):
        s_scr[g] = jnp.where(col <= row, s, NEG_BIG)

    def body(i, c):
        nxt = logits(qi - 1 - i)
        update(qi - i)
        for g in range(hg):
            s_scr[g] = nxt[g]
        return c

    lax.fori_loop(0, qi, body, 0)
    update(0)
    for g in range(hg):
        acc = acc_scr[g]
        o_ref[:, hcols(g)] = (acc[:, :HEAD_DIM] / acc[:, HEAD_DIM:]).astype(BF16)


def _fox_attn(y, aq, ak, *, batch, seq, n_heads, q_col, k_col, v_col, t, hg):
    n = y.shape[0]
    nq = seq // t
    wid = hg * HEAD_DIM
    kern = functools.partial(_fox_kernel, t=t, hg=hg)
    vmem = (10 * seq * wid * 2 + 4 * seq * wid * 2 + hg * 10 * t * t * 4
            + hg * 16 * t * 2 * HEAD_DIM * 4)
    return pl.pallas_call(
        kern,
        grid=(batch, n_heads // hg, nq),
        in_specs=[
            pl.BlockSpec((t, wid), lambda b, h, i: (b * nq + i, q_col + h)),
            pl.BlockSpec((t, wid), lambda b, h, i: (b * nq + i, h)),
            pl.BlockSpec((seq, wid), lambda b, h, i: (b, k_col + h)),
            pl.BlockSpec((seq, wid), lambda b, h, i: (b, h)),
            pl.BlockSpec((seq, wid), lambda b, h, i: (b, v_col + h)),
        ],
        out_specs=pl.BlockSpec((t, wid), lambda b, h, i: (b * nq + i, h)),
        out_shape=jax.ShapeDtypeStruct((n, n_heads * HEAD_DIM), BF16),
        scratch_shapes=[
            pltpu.VMEM((seq, 2 * wid), BF16),
            pltpu.VMEM((seq, 2 * wid), BF16),
            pltpu.VMEM((hg, t, t), F32),
            pltpu.VMEM((hg, t, LANES), F32),
            pltpu.VMEM((hg, t, 2 * HEAD_DIM), F32),
        ],
        compiler_params=_cparams(("parallel", "parallel", "arbitrary"), vmem),
        name="fox_attn",
    )(y, aq, y, ak, y)


def _sb_kernel(q_ref, k_ref, v_ref, o_ref, carry_scr, acc_scr, z_scr, *, t, hg):
    qi = pl.program_id(2)
    row = lax.broadcasted_iota(jnp.int32, (t, t), 0)
    col = lax.broadcasted_iota(jnp.int32, (t, t), 1)
    upper = (row > col).astype(BF16)
    reps = t // LANES

    def hcols(g):
        return slice(g * HEAD_DIM, (g + 1) * HEAD_DIM)

    carry_scr[...] = jnp.zeros(carry_scr.shape, F32)
    acc_scr[...] = jnp.zeros(acc_scr.shape, F32)

    def zlogits(ki):
        off = pl.multiple_of(ki * t, t)
        return [lax.dot_general(q_ref[:, hcols(g)], k_ref[pl.ds(off, t), hcols(g)], NT_DIMS,
                                preferred_element_type=F32) for g in range(hg)]

    def step(ki, masked, zs):
        off = pl.multiple_of(ki * t, t)
        for g in range(hg):
            v = v_ref[pl.ds(off, t), hcols(g)]
            z = zs[g]
            log_beta = _log_sigmoid(z)
            log_1m = log_beta - z
            if masked:
                keep = col < row
                log_1m = jnp.where(keep, log_1m, 0.0)
            hi = log_1m.astype(BF16)
            lo = (log_1m - hi.astype(F32)).astype(BF16)
            tail = (jnp.dot(hi, upper, preferred_element_type=F32)
                    + jnp.dot(lo, upper, preferred_element_type=F32))
            carry = carry_scr[g]
            a = jnp.exp(log_beta + tail + _lane_tile(carry, reps))
            if masked:
                a = jnp.where(keep, a, 0.0)
            acc_scr[g] += jnp.dot(a.astype(BF16), v, preferred_element_type=F32)
            carry_scr[g] = carry + jnp.sum(log_1m, axis=-1, keepdims=True)

    z_diag = zlogits(qi)
    for g, z in enumerate(zlogits(jnp.maximum(qi - 1, 0))):
        z_scr[g] = z
    step(qi, True, z_diag)

    def alive(c):
        i, worst = c
        return (i < qi) & (worst > SB_DEAD)

    def body(c):
        i, _ = c
        cur = [z_scr[g] for g in range(hg)]
        nxt = zlogits(jnp.maximum(qi - 2 - i, 0))
        step(qi - 1 - i, False, cur)
        for g in range(hg):
            z_scr[g] = nxt[g]
        return i + 1, jnp.max(carry_scr[...])

    lax.while_loop(alive, body, (jnp.int32(0), jnp.max(carry_scr[...])))
    for g in range(hg):
        o_ref[:, hcols(g)] = acc_scr[g].astype(BF16)


def _sb_attn(y, *, batch, seq, n_heads, q_col, k_col, v_col, t, hg):
    n = y.shape[0]
    nq = seq // t
    wid = hg * HEAD_DIM
    kern = functools.partial(_sb_kernel, t=t, hg=hg)
    vmem = 4 * seq * wid * 2 + hg * 14 * t * t * 4 + hg * 16 * t * LANES * 4
    return pl.pallas_call(
        kern,
        grid=(batch, n_heads // hg, nq),
        in_specs=[
            pl.BlockSpec((t, wid), lambda b, h, i: (b * nq + i, q_col + h)),
            pl.BlockSpec((seq, wid), lambda b, h, i: (b, k_col + h)),
            pl.BlockSpec((seq, wid), lambda b, h, i: (b, v_col + h)),
        ],
        out_specs=pl.BlockSpec((t, wid), lambda b, h, i: (b * nq + i, h)),
        out_shape=jax.ShapeDtypeStruct((n, n_heads * HEAD_DIM), BF16),
        scratch_shapes=[pltpu.VMEM((hg, t, LANES), F32), pltpu.VMEM((hg, t, HEAD_DIM), F32),
                        pltpu.VMEM((hg, t, t), F32)],
        compiler_params=_cparams(("parallel", "parallel", "arbitrary"), vmem),
        name="sb_attn",
    )(y, y, y)


def _merge_kernel(of_ref, os_ref, wf_ref, ws_ref, gf_ref, gs_ref, o_ref):
    a = jnp.dot(of_ref[...], wf_ref[...], preferred_element_type=F32)
    b = jnp.dot(os_ref[...], ws_ref[...], preferred_element_type=F32)
    o_ref[...] = (gf_ref[...].astype(F32) * a + gs_ref[...].astype(F32) * b).astype(BF16)


def _merge(o_fox, o_sb, w_of, w_os, y, *, gate_col, d, tm, tn):
    n, wf = o_fox.shape
    ws = o_sb.shape[1]
    nd = d // tn
    vmem = 2 * tm * (wf + ws) * 2 + 2 * (wf + ws) * tn * 2 + 6 * tm * tn * 2 + 4 * tm * tn * 4
    return pl.pallas_call(
        _merge_kernel,
        grid=(n // tm, nd),
        in_specs=[
            pl.BlockSpec((tm, wf), lambda i, j: (i, 0)),
            pl.BlockSpec((tm, ws), lambda i, j: (i, 0)),
            pl.BlockSpec((wf, tn), lambda i, j: (0, j)),
            pl.BlockSpec((ws, tn), lambda i, j: (0, j)),
            pl.BlockSpec((tm, tn), lambda i, j: (i, gate_col + j)),
            pl.BlockSpec((tm, tn), lambda i, j: (i, gate_col + nd + j)),
        ],
        out_specs=pl.BlockSpec((tm, tn), lambda i, j: (i, j)),
        out_shape=jax.ShapeDtypeStruct((n, d), BF16),
        compiler_params=_cparams(("parallel", "arbitrary"), vmem),
        name="merge",
    )(o_fox, o_sb, w_of, w_os, y, y)


def _out_proj_kernel(m_ref, w_ref, x_ref, o_ref):
    o_ref[...] = x_ref[...] + jnp.dot(m_ref[...], w_ref[...], preferred_element_type=F32)


def _out_proj(merged, w_out, x, *, tm, tn):
    n, d = merged.shape
    vmem = 2 * tm * d * 2 + 2 * d * tn * 2 + 6 * tm * tn * 4
    return pl.pallas_call(
        _out_proj_kernel,
        grid=(n // tm, d // tn),
        in_specs=[
            pl.BlockSpec((tm, d), lambda i, j: (i, 0)),
            pl.BlockSpec((d, tn), lambda i, j: (0, j)),
            pl.BlockSpec((tm, tn), lambda i, j: (i, j)),
        ],
        out_specs=pl.BlockSpec((tm, tn), lambda i, j: (i, j)),
        out_shape=jax.ShapeDtypeStruct((n, d), F32),
        compiler_params=_cparams(("parallel", "arbitrary"), vmem),
        name="out_proj",
    )(merged, w_out, x)


def _swiglu(h, w1, w3, w2):
    a = jnp.dot(h, w1, preferred_element_type=F32)
    b = jnp.dot(h, w3, preferred_element_type=F32)
    mid = (a * (1.0 / (1.0 + jnp.exp(-a))) * b).astype(BF16)
    return jnp.dot(mid, w2, preferred_element_type=F32)


def _swiglu_step(h, w1_ref, w3_ref, w2_ref):
    return _swiglu(h, w1_ref[...], w3_ref[...], w2_ref[...])


def _ffn_kernel(x_ref, g_ref, w1_ref, w3_ref, w2_ref, o_ref, h_scr):
    f = pl.program_id(1)

    @pl.when(f == 0)
    def _():
        _rms_to(h_scr, x_ref, g_ref)
        o_ref[...] = x_ref[...]

    o_ref[...] += _swiglu_step(h_scr[...], w1_ref, w3_ref, w2_ref)


def _ffn(x, g, w1, w3, w2, *, tm, tf):
    n, d = x.shape
    ff = w1.shape[1]
    vmem = 4 * tm * d * 4 + tm * d * 2 + 6 * d * tf * 2 + 5 * tm * tf * 4 + 2 * tm * d * 4
    return pl.pallas_call(
        _ffn_kernel,
        grid=(n // tm, ff // tf),
        in_specs=[
            pl.BlockSpec((tm, d), lambda i, f: (i, 0)),
            pl.BlockSpec((1, d), lambda i, f: (0, 0)),
            pl.BlockSpec((d, tf), lambda i, f: (0, f)),
            pl.BlockSpec((d, tf), lambda i, f: (0, f)),
            pl.BlockSpec((tf, d), lambda i, f: (f, 0)),
        ],
        out_specs=pl.BlockSpec((tm, d), lambda i, f: (i, 0)),
        out_shape=jax.ShapeDtypeStruct((n, d), F32),
        scratch_shapes=[pltpu.VMEM((tm, d), BF16)],
        compiler_params=_cparams(("parallel", "arbitrary"), vmem),
        name="ffn",
    )(x, g, w1, w3, w2)


def _router_kernel(x_ref, g_ref, whi_ref, wlo_ref, r_ref, *, n_experts):
    h = _rms_rows(x_ref[...], g_ref[...])
    h_hi = h.astype(BF16)
    h_lo = (h - h_hi.astype(F32)).astype(BF16)
    logits = (jnp.dot(h_hi, whi_ref[...], preferred_element_type=F32)
              + jnp.dot(h_hi, wlo_ref[...], preferred_element_type=F32)
              + jnp.dot(h_lo, whi_ref[...], preferred_element_type=F32))
    lane = lax.broadcasted_iota(jnp.int32, logits.shape, 1)
    logits = jnp.where(lane < n_experts, logits, NEG_BIG)
    m1 = jnp.max(logits, axis=-1, keepdims=True)
    i1 = jnp.min(jnp.where(logits == m1, lane, LANES), axis=-1, keepdims=True)
    rest = jnp.where(lane == i1, NEG_BIG, logits)
    m2 = jnp.max(rest, axis=-1, keepdims=True)
    i2 = jnp.min(jnp.where(rest == m2, lane, LANES), axis=-1, keepdims=True)
    e = jnp.exp(m2 - m1)
    w1 = 1.0 / (1.0 + e)
    w2 = e * w1
    out = jnp.where(lane == 0, i1.astype(F32),
          jnp.where(lane == 1, i2.astype(F32),
          jnp.where(lane == 2, w1, jnp.where(lane == 3, w2, 0.0))))
    r_ref[...] = out


def _router(x, g, w_hi, w_lo, *, n_experts, tm):
    n, d = x.shape
    kern = functools.partial(_router_kernel, n_experts=n_experts)
    vmem = 2 * tm * d * 4 + 4 * d * LANES * 2 + 4 * tm * d * 4 + 8 * tm * LANES * 4
    return pl.pallas_call(
        kern,
        grid=(n // tm,),
        in_specs=[
            pl.BlockSpec((tm, d), lambda i: (i, 0)),
            pl.BlockSpec((1, d), lambda i: (0, 0)),
            pl.BlockSpec((d, LANES), lambda i: (0, 0)),
            pl.BlockSpec((d, LANES), lambda i: (0, 0)),
        ],
        out_specs=pl.BlockSpec((tm, LANES), lambda i: (i, 0)),
        out_shape=jax.ShapeDtypeStruct((n, LANES), F32),
        compiler_params=_cparams(("parallel",), vmem),
        name="router",
    )(x, g, w_hi, w_lo)


def _experts_kernel(te_ref, nv_ref, nu_ref,
                    tok_ref, tokn_ref, dst_ref, g_ref, x_hbm, w1_ref, w3_ref, w2_ref,
                    out_hbm, xs_scr, h_scr, acc_scr, y_scr, gsem, ssem, *, tmx, sub):
    t = pl.program_id(0)
    f = pl.program_id(1)
    nf = pl.num_programs(1)
    nu = nu_ref[0]
    used = t < nu
    nv = nv_ref[t]

    def row_in(r, tok):
        return pltpu.make_async_copy(x_hbm.at[pl.ds(tok, 1), :], xs_scr.at[pl.ds(r, 1), :], gsem)

    def row_out(r, dst):
        return pltpu.make_async_copy(y_scr.at[pl.ds(r, 1), :], out_hbm.at[pl.ds(dst, 1), :], ssem)

    def gather(toks_ref):
        def issue(r, c):
            row_in(r, toks_ref[0, 0, r]).start()
            return c
        lax.fori_loop(0, tmx, issue, 0, unroll=8)

    def wait_rows(copy_of, count):
        def wait(r, c):
            copy_of(r).wait()
            return c
        if isinstance(count, int):
            lax.fori_loop(0, count, wait, 0, unroll=8)
        else:
            lax.fori_loop(0, count, wait, 0)

    @pl.when(used & (f == 0))
    def _():
        @pl.when(t == 0)
        def _():
            gather(tok_ref)
        wait_rows(lambda r: row_in(r, 0), tmx)
        _rms_to(h_scr, xs_scr, g_ref)
        acc_scr[...] = jnp.zeros_like(acc_scr)

    @pl.when(used & (f == 1) & (t + 1 < nu))
    def _():
        gather(tokn_ref)

    for c in range(tmx // sub):
        @pl.when(used & (nv > c * sub))
        def _(c=c):
            sl = slice(c * sub, (c + 1) * sub)
            acc_scr[sl, :] += _swiglu(h_scr[sl, :], w1_ref[...].astype(BF16),
                                      w3_ref[...].astype(BF16), w2_ref[...].astype(BF16))

    @pl.when(used & (f == nf - 1))
    def _():
        @pl.when(t > 0)
        def _():
            wait_rows(lambda r: row_out(r, 0), nv_ref[jnp.maximum(t - 1, 0)])
        y_scr[...] = acc_scr[...]

        def issue(r, c):
            row_out(r, dst_ref[0, 0, r]).start()
            return c
        lax.fori_loop(0, nv, issue, 0)

        @pl.when(t == nu - 1)
        def _():
            wait_rows(lambda r: row_out(r, 0), nv)


def _experts(x, g, tile_e, tile_nv, n_used, slot_tok, slot_dst, w1, w3, w2, *, layer, tmx, tf):
    n, d = x.shape
    n_tiles = tile_e.shape[0]
    ff = w1.shape[3]
    nf = ff // tf
    assert nf >= 2
    sub = _pick(tmx, 512)
    wbytes = w1.dtype.itemsize
    kern = functools.partial(_experts_kernel, tmx=tmx, sub=sub)

    def tclamp(t, nu):
        return jnp.minimum(t, nu[0] - 1)

    def fclamp(t, f, nu):
        return jnp.where(t < nu[0], f, nf - 1)

    def wmap(t, f, te, nv, nu):
        return (layer, te[tclamp(t, nu)], 0, fclamp(t, f, nu))

    grid_spec = pltpu.PrefetchScalarGridSpec(
        num_scalar_prefetch=3,
        grid=(n_tiles, nf),
        in_specs=[
            pl.BlockSpec((1, 1, tmx), lambda t, f, te, nv, nu: (tclamp(t, nu), 0, 0),
                         memory_space=pltpu.SMEM),
            pl.BlockSpec((1, 1, tmx), lambda t, f, te, nv, nu: (tclamp(t + 1, nu), 0, 0),
                         memory_space=pltpu.SMEM),
            pl.BlockSpec((1, 1, tmx), lambda t, f, te, nv, nu: (tclamp(t, nu), 0, 0),
                         memory_space=pltpu.SMEM),
            pl.BlockSpec((1, d), lambda t, f, te, nv, nu: (0, 0)),
            pl.BlockSpec(memory_space=pl.ANY),
            pl.BlockSpec((None, None, d, tf), wmap),
            pl.BlockSpec((None, None, d, tf), wmap),
            pl.BlockSpec((None, None, tf, d),
                         lambda t, f, te, nv, nu: (layer, te[tclamp(t, nu)], fclamp(t, f, nu), 0)),
        ],
        out_specs=pl.BlockSpec(memory_space=pl.ANY),
        scratch_shapes=[
            pltpu.VMEM((tmx, d), F32),
            pltpu.VMEM((tmx, d), BF16),
            pltpu.VMEM((tmx, d), F32),
            pltpu.VMEM((tmx, d), F32),
            pltpu.SemaphoreType.DMA(()),
            pltpu.SemaphoreType.DMA(()),
        ],
    )
    vmem = (3 * tmx * d * 4 + tmx * d * 2 + 6 * d * tf * wbytes + 3 * d * tf * 2
            + 5 * sub * tf * 4 + 2 * sub * d * 4 + 4 * NORM_ROWS * d * 4)
    return pl.pallas_call(
        kern,
        grid_spec=grid_spec,
        out_shape=jax.ShapeDtypeStruct((TOP_K * n, d), F32),
        compiler_params=_cparams(("arbitrary", "arbitrary"), vmem),
        name="experts",
    )(tile_e, tile_nv, n_used, slot_tok, slot_tok, slot_dst, g, x, w1, w3, w2)


def _combine_kernel(x_ref, r_ref, *refs):
    y_refs, o_ref = refs[:TOP_K], refs[TOP_K]
    acc = x_ref[...]
    for k in range(TOP_K):
        acc = acc + y_refs[k][...] * r_ref[:, TOP_K + k:TOP_K + k + 1]
    o_ref[...] = acc


def _combine(x, y2, route, *, tm):
    n, d = x.shape
    nb = n // tm
    return pl.pallas_call(
        _combine_kernel,
        grid=(nb,),
        in_specs=[
            pl.BlockSpec((tm, d), lambda i: (i, 0)),
            pl.BlockSpec((tm, LANES), lambda i: (i, 0)),
        ] + [pl.BlockSpec((tm, d), functools.partial(lambda i, k: (k * nb + i, 0), k=k))
             for k in range(TOP_K)],
        out_specs=pl.BlockSpec((tm, d), lambda i: (i, 0)),
        out_shape=jax.ShapeDtypeStruct((n, d), F32),
        compiler_params=_cparams(("parallel",), 12 * tm * d * 4),
        name="combine",
    )(x, route, *([y2] * TOP_K))


def _moe_plan(route, *, n_experts, tmx, n_tiles):
    n = route.shape[0]
    flat_e = route[:, :TOP_K].astype(jnp.int32).T.reshape(-1)
    onehot = (flat_e[:, None] == jnp.arange(n_experts, dtype=jnp.int32)[None, :]).astype(jnp.int32)
    rank = jnp.sum((jnp.cumsum(onehot, axis=0) - onehot) * onehot, axis=1)
    counts = jnp.sum(onehot, axis=0)
    padded = (counts + tmx - 1) // tmx * tmx
    pad_ends = jnp.cumsum(padded)
    pad_starts = pad_ends - padded
    dest = jnp.sum(onehot * pad_starts[None, :], axis=1) + rank
    slot_a = jnp.zeros((n_tiles * tmx,), jnp.int32).at[dest].set(
        jnp.arange(n * TOP_K, dtype=jnp.int32))
    tile_start = jnp.arange(n_tiles, dtype=jnp.int32) * tmx
    tile_e = jnp.minimum(jnp.sum((tile_start[:, None] >= pad_ends[None, :]).astype(jnp.int32), axis=1),
                         n_experts - 1)
    tile_nv = jnp.clip(counts[tile_e] - (tile_start - pad_starts[tile_e]), 0, tmx).astype(jnp.int32)
    n_used = (pad_ends[-1] // tmx).astype(jnp.int32).reshape(1)
    return (tile_e, tile_nv, n_used, (slot_a % n).reshape(n_tiles, 1, tmx),
            slot_a.reshape(n_tiles, 1, tmx))


def kernel(x, norm1_g, w_in, qn_g, kn_g, b_f, w_o_fox, w_o_sb, w_gate, b_gate, w_out, norm2_g,
           ffn_w1, ffn_w3, ffn_w2, w_router, moe_w1, moe_w3, moe_w2):
    batch, seq, d = x.shape
    depth = norm1_g.shape[0]
    h_fox = qn_g.shape[1]
    wfox = h_fox * HEAD_DIM
    wsb = (w_in.shape[2] - 3 * wfox - h_fox) // 3
    h_sb = wsb // HEAD_DIM
    n_experts = w_router.shape[2]
    n = batch * seq
    scale = HEAD_DIM ** -0.5

    tn = _pick(wfox, 1024)
    assert wsb % tn == 0 and d % tn == 0
    tm = _pick(n, 512)
    tm_proj = _pick(n, 1024)
    t_fox = _pick(seq, 512)
    t_sb = _pick(seq, 256)
    tf = _pick(ffn_w1.shape[2], 512)
    tfx = _pick(moe_w1.shape[3], 256)
    tmx = _pick(n, 1024)
    n_tiles = (n * TOP_K) // tmx + n_experts

    off_f = 3 * wfox
    n_proj = 3 * wfox + 3 * wsb
    hg = 2 if h_fox % 2 == 0 and h_sb % 2 == 0 else 1
    hg_sb = 4 if hg == 2 and h_sb % 4 == 0 and wfox % (4 * HEAD_DIM) == 0 else hg
    hcols = hg * HEAD_DIM
    hcols_sb = hg_sb * HEAD_DIM

    xf = x.reshape(n, d)
    w_cat, wf_pad = _wprep(w_in, w_gate, off_f=off_f, n_f=h_fox, tr=_pick(d, 128))
    for layer in range(depth):
        bf_pad = jnp.zeros((1, LANES), F32).at[0, :h_fox].set(b_f[layer])
        cvec = jnp.concatenate([
            qn_g[layer].reshape(-1) * (scale * LOG2E), kn_g[layer].reshape(-1),
            jnp.ones((wfox,), F32),
            jnp.full((wsb,), scale, F32), jnp.ones((2 * wsb,), F32),
            jnp.ones((2 * d,), F32)]).reshape(1, -1)
        bvec = jnp.concatenate([jnp.zeros((n_proj,), F32), b_gate[layer]]).reshape(1, -1)

        y, logf = _proj(xf, norm1_g[layer].reshape(1, d), w_cat, wf_pad, cvec, bvec, bf_pad,
                        layer=layer, n_qk_tiles=2 * wfox // tn,
                        n_plain_tiles=(wfox + 3 * wsb) // tn, tm=tm_proj, tn=tn)
        aq, ak = _dcum(logf, batch=batch, seq=seq, n_heads=h_fox)
        o_fox = _fox_attn(y, aq, ak, batch=batch, seq=seq, n_heads=h_fox,
                          q_col=0, k_col=wfox // hcols, v_col=2 * wfox // hcols, t=t_fox, hg=hg)
        sb0 = 3 * wfox // hcols_sb
        o_sb = _sb_attn(y, batch=batch, seq=seq, n_heads=h_sb, q_col=sb0,
                        k_col=sb0 + wsb // hcols_sb, v_col=sb0 + 2 * wsb // hcols_sb,
                        t=t_sb, hg=hg_sb)
        merged = _merge(o_fox, o_sb, w_o_fox[layer].astype(BF16), w_o_sb[layer].astype(BF16), y,
                        gate_col=n_proj // tn, d=d, tm=tm_proj, tn=tn)
        xf = _out_proj(merged, w_out[layer].astype(BF16), xf, tm=tm_proj, tn=tn)

        i = layer // 2
        g2 = norm2_g[layer].reshape(1, d)
        if layer % 2 == 0:
            xf = _ffn(xf, g2, ffn_w1[i].astype(BF16), ffn_w3[i].astype(BF16),
                      ffn_w2[i].astype(BF16), tm=tm, tf=tf)
        else:
            wr = jnp.zeros((d, LANES), F32).at[:, :n_experts].set(w_router[i])
            wr_hi = wr.astype(BF16)
            wr_lo = (wr - wr_hi.astype(F32)).astype(BF16)
            route = _router(xf, g2, wr_hi, wr_lo, n_experts=n_experts, tm=tm)
            plan = _moe_plan(route, n_experts=n_experts, tmx=tmx, n_tiles=n_tiles)
            y2 = _experts(xf, g2, *plan, moe_w1, moe_w3, moe_w2, layer=i, tmx=tmx, tf=tfx)
            xf = _combine(xf, y2, route, tm=tm)
    return xf.reshape(batch, seq, d)
```
